```python
import functools
import jax, jax.numpy as jnp
from jax import lax
import numpy as np

D_MODEL = 2048
BATCH = 16
SEQ = 2048
DEPTH = 1
DEC_BATCH = 32
DEC_SEQ = 4
PAST_LEN = 16384
PAGE_SIZE = 128

N_META = 16
POOL_WIDTH = D_MODEL // 2
POOL_WINDOWS = (2, 4, 8, 16)
N_POOL_GROUPS = len(POOL_WINDOWS)
POOL_GROUP = POOL_WIDTH // N_POOL_GROUPS
POOL_HIST = max(POOL_WINDOWS) - 1
HEAD_DIM = 128
N_HEADS = (D_MODEL // 2) // HEAD_DIM
ATTN_WIDTH = N_HEADS * HEAD_DIM
N_BRANCH = 2
D_FF = ((8 * D_MODEL // 3 + 127) // 128) * 128
Q_BLOCK = 128
EPS = 1e-6
SB_BIAS_INIT = -9.0
IN_WIDTH = POOL_WIDTH + 3 * ATTN_WIDTH + N_BRANCH * D_MODEL
SPLITS = (POOL_WIDTH, POOL_WIDTH + ATTN_WIDTH, POOL_WIDTH + 2 * ATTN_WIDTH, POOL_WIDTH + 3 * ATTN_WIDTH)

kernel_name = 'hybrid_pool_stickbreak_macaron_step'


def rms_norm(x, g):
    xf = x.astype(jnp.float32)
    y = xf * lax.rsqrt(jnp.mean(xf * xf, axis=-1, keepdims=True) + EPS)
    return (y * g.astype(jnp.float32)).astype(x.dtype)


def swiglu_half(x, g, w_gu, w_down):
    a, b = jnp.split(rms_norm(x, g) @ w_gu, 2, axis=-1)
    return x + 0.5 * ((jax.nn.silu(a) * b) @ w_down)


def in_projection(x, g_mix, w_in, g_q, g_k):
    p = rms_norm(x, g_mix) @ w_in
    u, q, k, v, gates = jnp.split(p, SPLITS, axis=-1)
    hs = x.shape[:-1] + (N_HEADS, HEAD_DIM)
    q = rms_norm(q.reshape(hs), g_q)
    k = rms_norm(k.reshape(hs), g_k)
    return u, q, k, v.reshape(hs), gates


def pool_mixer(u_ext, pos0, w_pool, pool_scale):
    b, n, c = u_ext.shape
    t = n - POOL_HIST
    uf = u_ext.astype(jnp.float32)
    csum = jnp.concatenate([jnp.zeros((b, 1, c), jnp.float32), jnp.cumsum(uf, axis=1)], axis=1)
    pos = pos0 + jnp.arange(t)
    u_new = uf[:, POOL_HIST:]
    parts = []
    for gi, w in enumerate(POOL_WINDOWS):
        cols = slice(gi * POOL_GROUP, (gi + 1) * POOL_GROUP)
        win_sum = csum[:, POOL_HIST + 1:POOL_HIST + 1 + t, cols] - csum[:, POOL_HIST + 1 - w:POOL_HIST + 1 - w + t, cols]
        count = jnp.minimum(w, pos + 1).astype(jnp.float32)[None, :, None]
        parts.append(win_sum / count - u_new[:, :, cols])
    d = jnp.stack(parts, axis=2).astype(u_ext.dtype)
    y = jnp.einsum('btgc,gcd->btgd', d, w_pool).reshape(b, t, POOL_WIDTH)
    return y * pool_scale


def stick_breaking(q, k, v, q_pos, k_pos, sb_bias):
    z = jnp.einsum('bqhd,bkhd->bhqk', q, k, preferred_element_type=jnp.float32) * (HEAD_DIM ** -0.5)
    z = z + sb_bias.astype(jnp.float32)[None, :, None, None]
    visible = k_pos[None, :] < q_pos[:, None]
    log_stay = jnp.where(visible, jax.nn.log_sigmoid(-z), 0.0)
    later = lax.cumsum(log_stay, axis=3, reverse=True) - log_stay
    wts = jnp.where(visible, jnp.exp(jax.nn.log_sigmoid(z) + later), 0.0)
    return jnp.einsum('bhqk,bkhd->bqhd', wts.astype(v.dtype), v)


def prompt_attention(q, k, v, sb_bias):
    b, L = q.shape[0], q.shape[1]
    seq = L - N_META
    meta_pos = jnp.arange(N_META)
    o_meta = stick_breaking(q[:, :N_META], k[:, :N_META], v[:, :N_META], meta_pos, meta_pos, sb_bias)
    nb = seq // Q_BLOCK
    qb = q[:, N_META:].reshape(b, nb, Q_BLOCK, N_HEADS, HEAD_DIM).transpose(1, 0, 2, 3, 4)
    qpos = (N_META + jnp.arange(seq)).reshape(nb, Q_BLOCK)
    k_pos = jnp.arange(L)
    ob = lax.map(lambda a: stick_breaking(a[0], k, v, a[1], k_pos, sb_bias), (qb, qpos))
    o_real = ob.transpose(1, 0, 2, 3, 4).reshape(b, seq, N_HEADS, HEAD_DIM)
    return jnp.concatenate([o_meta, o_real], axis=1)


def sample_attention(q, k, v, sb_bias, k_past, v_past):
    k_all = jnp.concatenate([k_past.astype(k.dtype), k], axis=1)
    v_all = jnp.concatenate([v_past.astype(v.dtype), v], axis=1)
    past = k_past.shape[1]
    t = q.shape[1]
    return stick_breaking(q, k_all, v_all, past + jnp.arange(t), jnp.arange(past + t), sb_bias)


def merge_out(x, pool_out, attn_out, gates, w_bp, w_ba, w_out):
    g_pool, g_attn = jnp.split(jax.nn.sigmoid(gates), 2, axis=-1)
    attn_flat = attn_out.reshape(attn_out.shape[:2] + (ATTN_WIDTH,))
    m = g_pool * (pool_out @ w_bp) + g_attn * (attn_flat @ w_ba)
    return x + m @ w_out


def decoder_layer(x, pool_hist, pos0, attend, g1, w1gu, w1d, gm, w_in, gq, gk, sbb, w_pool, pscale, wbp, wba, wo, g2, w2gu, w2d):
    x = swiglu_half(x, g1, w1gu, w1d)
    u, q, k, v, gates = in_projection(x, gm, w_in, gq, gk)
    u_ext = jnp.concatenate([pool_hist.astype(u.dtype), u], axis=1)
    a = pool_mixer(u_ext, pos0, w_pool, pscale)
    o = attend(q, k, v, sbb)
    x = merge_out(x, a, o, gates, wbp, wba, wo)
    x = swiglu_half(x, g2, w2gu, w2d)
    return x, k, v, u_ext[:, -POOL_HIST:]


def setup_inputs(seed: int = 0) -> dict:
    key = jax.random.key(seed)
    ks = jax.random.split(key, 24)
    f32 = jnp.float32
    n_pages = PAST_LEN // PAGE_SIZE
    n_used = DEC_BATCH * n_pages
    n_pool = n_used + n_used // 4

    def nrm(k, shape, scale):
        return jax.random.normal(k, shape, f32) * scale

    def gain(k, shape):
        return 1.0 + 0.02 * jax.random.normal(k, shape, f32)

    page_table = jax.random.permutation(ks[5], n_pool)[:n_used].reshape(DEC_BATCH, n_pages).astype(jnp.int32)
    return {
        'x_prompt': nrm(ks[0], (BATCH, SEQ, D_MODEL), 1.0),
        'x_sample': nrm(ks[1], (DEC_BATCH, DEC_SEQ, D_MODEL), 1.0),
        'cache_k': nrm(ks[2], (DEPTH, n_pool, PAGE_SIZE, N_HEADS, HEAD_DIM), 1.0),
        'cache_v': nrm(ks[3], (DEPTH, n_pool, PAGE_SIZE, N_HEADS, HEAD_DIM), 1.0),
        'state_pool': nrm(ks[4], (DEPTH, DEC_BATCH, POOL_HIST, POOL_WIDTH), 1.0),
        'page_table': page_table,
        'meta_tokens': nrm(ks[6], (N_META, D_MODEL), 1.0),
        'g_ffn1': gain(ks[7], (DEPTH, D_MODEL)),
        'w_ffn1_gu': nrm(ks[8], (DEPTH, D_MODEL, 2 * D_FF), D_MODEL ** -0.5),
        'w_ffn1_down': nrm(ks[9], (DEPTH, D_FF, D_MODEL), D_FF ** -0.5),
        'g_mix': gain(ks[10], (DEPTH, D_MODEL)),
        'w_in': nrm(ks[11], (DEPTH, D_MODEL, IN_WIDTH), D_MODEL ** -0.5),
        'g_q': gain(ks[12], (DEPTH, HEAD_DIM)),
        'g_k': gain(ks[13], (DEPTH, HEAD_DIM)),
        'sb_bias': SB_BIAS_INIT + 0.1 * jax.random.normal(ks[22], (DEPTH, N_HEADS), f32),
        'w_pool': nrm(ks[14], (DEPTH, N_POOL_GROUPS, POOL_GROUP, POOL_GROUP), POOL_GROUP ** -0.5),
        'pool_scale': gain(ks[15], (DEPTH, POOL_WIDTH)),
        'w_branch_pool': nrm(ks[16], (DEPTH, POOL_WIDTH, D_MODEL), POOL_WIDTH ** -0.5),
        'w_branch_attn': nrm(ks[17], (DEPTH, ATTN_WIDTH, D_MODEL), ATTN_WIDTH ** -0.5),
        'w_out': nrm(ks[18], (DEPTH, D_MODEL, D_MODEL), D_MODEL ** -0.5),
        'g_ffn2': gain(ks[19], (DEPTH, D_MODEL)),
        'w_ffn2_gu': nrm(ks[20], (DEPTH, D_MODEL, 2 * D_FF), D_MODEL ** -0.5),
        'w_ffn2_down': nrm(ks[21], (DEPTH, D_FF, D_MODEL), D_FF ** -0.5),
    }


def reference(x_prompt, x_sample, cache_k, cache_v, state_pool, page_table, meta_tokens, g_ffn1, w_ffn1_gu, w_ffn1_down, g_mix, w_in, g_q, g_k, sb_bias, w_pool, pool_scale, w_branch_pool, w_branch_attn, w_out, g_ffn2, w_ffn2_gu, w_ffn2_down):
    b = x_prompt.shape[0]
    db, dt = x_sample.shape[0], x_sample.shape[1]
    past_len = page_table.shape[1] * PAGE_SIZE
    meta = jnp.broadcast_to(meta_tokens[None].astype(x_prompt.dtype), (b, N_META, D_MODEL))
    xp = jnp.concatenate([meta, x_prompt], axis=1)
    xs = x_sample
    k_p, v_p, pool_p, k_s, v_s, pool_s = [], [], [], [], [], []
    for l in range(DEPTH):
        w = (g_ffn1[l], w_ffn1_gu[l], w_ffn1_down[l], g_mix[l], w_in[l], g_q[l], g_k[l], sb_bias[l], w_pool[l], pool_scale[l],
             w_branch_pool[l], w_branch_attn[l], w_out[l], g_ffn2[l], w_ffn2_gu[l], w_ffn2_down[l])
        hist0 = jnp.zeros((b, POOL_HIST, POOL_WIDTH), xp.dtype)
        xp, kp, vp, pp = decoder_layer(xp, hist0, 0, prompt_attention, *w)
        k_p.append(kp)
        v_p.append(vp)
        pool_p.append(pp)
        k_past = cache_k[l][page_table].reshape(db, past_len, N_HEADS, HEAD_DIM)
        v_past = cache_v[l][page_table].reshape(db, past_len, N_HEADS, HEAD_DIM)
        attend_s = functools.partial(sample_attention, k_past=k_past, v_past=v_past)
        xs, ks_, vs_, ps_ = decoder_layer(xs, state_pool[l], past_len, attend_s, *w)
        k_s.append(ks_)
        v_s.append(vs_)
        pool_s.append(ps_)
    y_prompt = xp[:, N_META:]
    y_sample = xs
    return (y_prompt, y_sample, jnp.stack(k_p), jnp.stack(v_p), jnp.stack(pool_p), jnp.stack(k_s), jnp.stack(v_s), jnp.stack(pool_s))
```

```python
import functools

import jax
import jax.numpy as jnp
from jax import lax
from jax.experimental import pallas as pl
from jax.experimental.pallas import tpu as pltpu

F32 = jnp.float32
BF16 = jnp.bfloat16

EPS = 1e-6
N_META = 16
POOL_WINDOWS = (2, 4, 8, 16)
POOL_HIST = max(POOL_WINDOWS) - 1
HIST_ROWS = 16
PAGE_SIZE = 128
PAGES_PER_STEP = 4
FF_CHUNK = 512
ROW_TILE = 512
ATTN_TILE = 256
VMEM_LIMIT = 56 * 1024 * 1024


def _cparams(sem):
    return pltpu.CompilerParams(dimension_semantics=sem, vmem_limit_bytes=VMEM_LIMIT)


def _row_tile(t):
    return t if t <= ROW_TILE else ROW_TILE


def _rms_rows(x, g):
    ms = jnp.mean(x * x, axis=-1, keepdims=True)
    return x * lax.rsqrt(ms + EPS) * g


def _ffn_kernel(x_ref, g_ref, wgu_ref, wd_ref, o_ref, xn_ref, *, tf, n_ff):
    j = pl.program_id(1)

    @pl.when(j == 0)
    def _():
        xn_ref[...] = _rms_rows(x_ref[...], g_ref[...]).astype(BF16)
        o_ref[...] = jnp.zeros_like(o_ref)

    h = jnp.dot(xn_ref[...], wgu_ref[...], preferred_element_type=F32)
    a = h[:, :tf]
    b = h[:, tf:]
    act = (a * jax.nn.sigmoid(a)) * b
    o_ref[...] += jnp.dot(act.astype(BF16), wd_ref[...], preferred_element_type=F32)

    @pl.when(j == n_ff - 1)
    def _():
        o_ref[...] = x_ref[...] + 0.5 * o_ref[...]


def _ffn_half(x, g, wgu, wd):
    t, d = x.shape
    tf = FF_CHUNK
    n_ff = wd.shape[0] // tf
    tm = _row_tile(t)
    return pl.pallas_call(
        functools.partial(_ffn_kernel, tf=tf, n_ff=n_ff),
        grid=(pl.cdiv(t, tm), n_ff),
        in_specs=[
            pl.BlockSpec((tm, d), lambda i, j: (i, 0)),
            pl.BlockSpec((1, d), lambda i, j: (0, 0)),
            pl.BlockSpec((d, 2 * tf), lambda i, j: (0, j)),
            pl.BlockSpec((tf, d), lambda i, j: (j, 0)),
        ],
        out_specs=pl.BlockSpec((tm, d), lambda i, j: (i, 0)),
        out_shape=jax.ShapeDtypeStruct((t, d), F32),
        scratch_shapes=[pltpu.VMEM((tm, d), BF16)],
        compiler_params=_cparams(("parallel", "arbitrary")),
        name="ffn_half",
    )(x, g.reshape(1, d), wgu, wd)


def _prep_ffn_weights(w_gu, w_down):
    d, two_f = w_gu.shape
    f = two_f // 2
    tf = FF_CHUNK
    n_ff = pl.cdiv(f, tf)
    pad = n_ff * tf - f
    wg = jnp.pad(w_gu[:, :f].astype(BF16), ((0, 0), (0, pad))).reshape(d, n_ff, 1, tf)
    wu = jnp.pad(w_gu[:, f:].astype(BF16), ((0, 0), (0, pad))).reshape(d, n_ff, 1, tf)
    wgu = jnp.concatenate([wg, wu], axis=2).reshape(d, n_ff * 2 * tf)
    wd = jnp.pad(w_down.astype(BF16), ((0, pad), (0, 0)))
    return wgu, wd


def _inproj_kernel(x_ref, g_ref, w_ref, hg_ref, o_ref, xn_ref, *, n_heads, hd):
    j = pl.program_id(1)

    @pl.when(j == 0)
    def _():
        xn_ref[...] = _rms_rows(x_ref[...], g_ref[...]).astype(BF16)

    p = jnp.dot(xn_ref[...], w_ref[...], preferred_element_type=F32)
    is_qk = jnp.logical_or(j == 1, j == 2)

    @pl.when(is_qk)
    def _():
        for h in range(n_heads):
            cols = slice(h * hd, (h + 1) * hd)
            o_ref[:, cols] = _rms_rows(p[:, cols], hg_ref[:, cols])

    @pl.when(jnp.logical_not(is_qk))
    def _():
        o_ref[...] = p


def _in_proj(x, g_mix, w_in, head_gain, n_heads, hd):
    t, d = x.shape
    in_w = w_in.shape[1]
    tn = n_heads * hd
    tm = _row_tile(t)
    return pl.pallas_call(
        functools.partial(_inproj_kernel, n_heads=n_heads, hd=hd),
        grid=(pl.cdiv(t, tm), in_w // tn),
        in_specs=[
            pl.BlockSpec((tm, d), lambda i, j: (i, 0)),
            pl.BlockSpec((1, d), lambda i, j: (0, 0)),
            pl.BlockSpec((d, tn), lambda i, j: (0, j)),
            pl.BlockSpec((None, 1, tn), lambda i, j: (j, 0, 0)),
        ],
        out_specs=pl.BlockSpec((tm, tn), lambda i, j: (i, j)),
        out_shape=jax.ShapeDtypeStruct((t, in_w), F32),
        scratch_shapes=[pltpu.VMEM((tm, d), BF16)],
        compiler_params=_cparams(("parallel", "arbitrary")),
        name="in_proj",
    )(x, g_mix.reshape(1, d), w_in, head_gain)


def _pool_kernel(u_ref, halo_ref, hist_ref, wp_ref, ps_ref, o_ref, *, tm, pos0, group):
    i = pl.program_id(1)
    prev = jnp.where(i == 0, hist_ref[...], halo_ref[...])
    pos = pos0 + i * tm + lax.broadcasted_iota(jnp.int32, (tm, 1), 0)
    for gi, w in enumerate(POOL_WINDOWS):
        cols = slice(gi * group, (gi + 1) * group)
        x = jnp.concatenate([prev[:, cols], u_ref[:, cols]], axis=0)
        s = x
        span = 1
        while span < w:
            s = s + pltpu.roll(s, span, axis=0)
            span *= 2
        count = jnp.minimum(w, pos + 1).astype(F32)
        d = s[HIST_ROWS:] / count - x[HIST_ROWS:]
        y = jnp.dot(d.astype(BF16), wp_ref[gi], preferred_element_type=F32)
        o_ref[:, cols] = (y * ps_ref[:, cols]).astype(o_ref.dtype)


def _pool_mixer(p3, hist, w_pool, pool_scale, pos0, pw):
    b, s, _ = p3.shape
    tm = s if s <= ROW_TILE else ROW_TILE
    n_groups = w_pool.shape[0]
    group = pw // n_groups
    halo_blocks = tm // HIST_ROWS
    per_batch = hist.shape[0] != 1
    return pl.pallas_call(
        functools.partial(_pool_kernel, tm=tm, pos0=pos0, group=group),
        grid=(b, s // tm),
        in_specs=[
            pl.BlockSpec((None, tm, pw), lambda bi, i: (bi, i, 0)),
            pl.BlockSpec((None, HIST_ROWS, pw), lambda bi, i: (bi, jnp.maximum(i * halo_blocks - 1, 0), 0)),
            pl.BlockSpec((None, HIST_ROWS, pw), (lambda bi, i: (bi, 0, 0)) if per_batch else (lambda bi, i: (0, 0, 0))),
            pl.BlockSpec((n_groups, group, group), lambda bi, i: (0, 0, 0)),
            pl.BlockSpec((1, pw), lambda bi, i: (0, 0)),
        ],
        out_specs=pl.BlockSpec((None, tm, pw), lambda bi, i: (bi, i, 0)),
        out_shape=jax.ShapeDtypeStruct((b, s, pw), BF16),
        compiler_params=_cparams(("parallel", "arbitrary")),
        name="pool_mixer",
    )(p3, p3, hist, w_pool, pool_scale.reshape(1, pw))


def _sb_block(qb, kb, vb, tri, bias, scale, c, acc, mask):
    s = lax.dot_general(qb, kb, (((1,), (1,)), ((), ())), preferred_element_type=F32)
    z = s * scale + bias
    l = jnp.log1p(jnp.exp(-jnp.abs(z)))
    log_stay = -jnp.maximum(z, 0.0) - l
    log_beta = jnp.minimum(z, 0.0) - l
    if mask is not None:
        log_stay = jnp.where(mask, log_stay, 0.0)
    hi = log_stay.astype(BF16)
    lo = (log_stay - hi.astype(F32)).astype(BF16)
    later = (jnp.dot(hi, tri, preferred_element_type=F32)
             + jnp.dot(lo, tri, preferred_element_type=F32))
    w = jnp.exp(log_beta + later + c)
    if mask is not None:
        w = jnp.where(mask, w, 0.0)
    acc = acc + jnp.dot(w.astype(BF16), vb, preferred_element_type=F32)
    c = c + later[:, :1] + log_stay[:, :1]
    return c, acc


def _pattn_kernel(bias_ref, q_ref, k_ref, v_ref, km_ref, vm_ref, tri_ref, o_ref,
                  kb_ref, vb_ref, kmb_ref, vmb_ref, *, tq, hd, n_meta):
    h = pl.program_id(1)
    i = pl.program_id(2)
    scale = hd ** -0.5
    bias = bias_ref[h]

    @pl.when(i == 0)
    def _():
        kb_ref[...] = k_ref[...].astype(BF16)
        vb_ref[...] = v_ref[...].astype(BF16)
        kmb_ref[...] = km_ref[...].astype(BF16)
        vmb_ref[...] = vm_ref[...].astype(BF16)

    qb = q_ref[...].astype(BF16)
    tri = tri_ref[...]
    c = jnp.zeros((tq, 1), F32)
    acc = jnp.zeros((tq, hd), F32)

    row = lax.broadcasted_iota(jnp.int32, (tq, tq), 0)
    col = lax.broadcasted_iota(jnp.int32, (tq, tq), 1)
    start = pl.multiple_of(i * tq, tq)
    c, acc = _sb_block(qb, kb_ref[pl.ds(start, tq), :], vb_ref[pl.ds(start, tq), :], tri, bias, scale,
                       c, acc, col < row)

    def body(t, carry):
        c, acc = carry
        st = pl.multiple_of((i - 1 - t) * tq, tq)
        return _sb_block(qb, kb_ref[pl.ds(st, tq), :], vb_ref[pl.ds(st, tq), :], tri, bias, scale, c, acc, None)

    c, acc = lax.fori_loop(0, i, body, (c, acc))

    nm = kmb_ref.shape[0]
    mcol = lax.broadcasted_iota(jnp.int32, (tq, nm), 1)
    c, acc = _sb_block(qb, kmb_ref[...], vmb_ref[...], tri[:nm, :nm], bias, scale, c, acc, mcol < n_meta)
    o_ref[...] = acc.astype(o_ref.dtype)


def _prompt_attention(p3, pm_pad, sb_bias, n_heads, hd, n_meta):
    b, s, _ = p3.shape
    aw = n_heads * hd
    tq = min(ATTN_TILE, s)
    nm = pm_pad.shape[0]
    qo, ko, vo = aw // hd, 2 * aw // hd, 3 * aw // hd
    tri = (lax.broadcasted_iota(jnp.int32, (tq, tq), 0) > lax.broadcasted_iota(jnp.int32, (tq, tq), 1)).astype(BF16)
    return pl.pallas_call(
        functools.partial(_pattn_kernel, tq=tq, hd=hd, n_meta=n_meta),
        grid=(b, n_heads, s // tq),
        in_specs=[
            pl.BlockSpec(memory_space=pltpu.SMEM),
            pl.BlockSpec((None, tq, hd), lambda bi, h, i: (bi, i, qo + h)),
            pl.BlockSpec((None, s, hd), lambda bi, h, i: (bi, 0, ko + h)),
            pl.BlockSpec((None, s, hd), lambda bi, h, i: (bi, 0, vo + h)),
            pl.BlockSpec((nm, hd), lambda bi, h, i: (0, ko + h)),
            pl.BlockSpec((nm, hd), lambda bi, h, i: (0, vo + h)),
            pl.BlockSpec((tq, tq), lambda bi, h, i: (0, 0)),
        ],
        out_specs=pl.BlockSpec((None, tq, hd), lambda bi, h, i: (bi, i, h)),
        out_shape=jax.ShapeDtypeStruct((b, s, aw), BF16),
        scratch_shapes=[pltpu.VMEM((s, hd), BF16), pltpu.VMEM((s, hd), BF16),
                        pltpu.VMEM((nm, hd), BF16), pltpu.VMEM((nm, hd), BF16)],
        compiler_params=_cparams(("parallel", "arbitrary", "arbitrary")),
        name="prompt_attn",
    )(sb_bias, p3, p3, p3, pm_pad, pm_pad, tri)


def _sattn_kernel(pt_ref, qbd_ref, bias_ref, kn_ref, vn_ref, tri_ref, *rest, hd, n_heads, dt, n_steps):
    page_refs = rest[:2 * PAGES_PER_STEP]
    o_ref, c_ref, acc_ref = rest[2 * PAGES_PER_STEP:]
    s = pl.program_id(1)
    scale = hd ** -0.5
    qb = qbd_ref[...]
    bias = bias_ref[...]
    tri = tri_ref[...]
    m = qb.shape[0]

    @pl.when(s == 0)
    def _():
        n_new = kn_ref.shape[0]
        zpad = jnp.zeros((PAGE_SIZE - n_new, kn_ref.shape[1]), BF16)
        kn = jnp.concatenate([kn_ref[...].astype(BF16), zpad], axis=0)
        vn = jnp.concatenate([vn_ref[...].astype(BF16), zpad], axis=0)
        qi = lax.rem(lax.broadcasted_iota(jnp.int32, (m, PAGE_SIZE), 0), dt)
        col = lax.broadcasted_iota(jnp.int32, (m, PAGE_SIZE), 1)
        c, acc = _sb_block(qb, kn, vn, tri, bias, scale, jnp.zeros((m, 1), F32),
                           jnp.zeros((m, acc_ref.shape[1]), F32), col < qi)
        c_ref[...] = c
        acc_ref[...] = acc

    c = c_ref[...]
    acc = acc_ref[...]
    for r in range(PAGES_PER_STEP):
        kb = page_refs[r][...].astype(BF16)
        vb = page_refs[PAGES_PER_STEP + r][...].astype(BF16)
        c, acc = _sb_block(qb, kb, vb, tri, bias, scale, c, acc, None)
    c_ref[...] = c
    acc_ref[...] = acc

    @pl.when(s == n_steps - 1)
    def _():
        for h in range(n_heads):
            o_ref[h * dt:(h + 1) * dt, :] = acc[h * dt:(h + 1) * dt, h * hd:(h + 1) * hd]


def _sample_attention(q, k_new, v_new, cache_k, cache_v, page_table, sb_bias, n_heads, hd):
    db, dt, aw = q.shape
    n_pages = page_table.shape[1]
    n_steps = n_pages // PAGES_PER_STEP
    m = n_heads * dt
    q4 = q.reshape(db, dt, n_heads, hd).transpose(0, 2, 1, 3)
    eye = jnp.eye(n_heads, dtype=F32)
    qbd = (q4[:, :, :, None, :] * eye[None, :, None, :, None]).reshape(db, m, aw).astype(BF16)
    bias_rows = jnp.repeat(sb_bias, dt).reshape(m, 1)
    new_rows = 8 * pl.cdiv(dt, 8)
    kn = jnp.pad(k_new, ((0, 0), (0, new_rows - dt), (0, 0)))
    vn = jnp.pad(v_new, ((0, 0), (0, new_rows - dt), (0, 0)))
    tri = (lax.broadcasted_iota(jnp.int32, (PAGE_SIZE, PAGE_SIZE), 0)
           > lax.broadcasted_iota(jnp.int32, (PAGE_SIZE, PAGE_SIZE), 1)).astype(BF16)

    def page_spec(r):
        def index(bi, s, pt):
            return (pt[bi * n_pages + (n_pages - 1 - (s * PAGES_PER_STEP + r))], 0, 0)
        return pl.BlockSpec((None, PAGE_SIZE, aw), index)

    grid_spec = pltpu.PrefetchScalarGridSpec(
        num_scalar_prefetch=1,
        grid=(db, n_steps),
        in_specs=[
            pl.BlockSpec((None, m, aw), lambda bi, s, pt: (bi, 0, 0)),
            pl.BlockSpec((m, 1), lambda bi, s, pt: (0, 0)),
            pl.BlockSpec((None, new_rows, aw), lambda bi, s, pt: (bi, 0, 0)),
            pl.BlockSpec((None, new_rows, aw), lambda bi, s, pt: (bi, 0, 0)),
            pl.BlockSpec((PAGE_SIZE, PAGE_SIZE), lambda bi, s, pt: (0, 0)),
        ] + [page_spec(r) for r in range(PAGES_PER_STEP)] * 2,
        out_specs=pl.BlockSpec((None, m, hd), lambda bi, s, pt: (bi, 0, 0)),
        scratch_shapes=[pltpu.VMEM((m, 1), F32), pltpu.VMEM((m, aw), F32)],
    )
    out = pl.pallas_call(
        functools.partial(_sattn_kernel, hd=hd, n_heads=n_heads, dt=dt, n_steps=n_steps),
        grid_spec=grid_spec,
        out_shape=jax.ShapeDtypeStruct((db, m, hd), F32),
        compiler_params=_cparams(("parallel", "arbitrary")),
        name="sample_attn",
    )(page_table.reshape(-1), qbd, bias_rows, kn, vn, tri,
      *([cache_k] * PAGES_PER_STEP), *([cache_v] * PAGES_PER_STEP))
    return out.reshape(db, n_heads, dt, hd).transpose(0, 2, 1, 3).reshape(db, dt, aw)


def _merge_kernel(x_ref, a_ref, o_ref, gp_ref, ga_ref, wbp_ref, wba_ref, wo_ref, y_ref):
    bp = jnp.dot(a_ref[...].astype(BF16), wbp_ref[...], preferred_element_type=F32)
    ba = jnp.dot(o_ref[...].astype(BF16), wba_ref[...], preferred_element_type=F32)
    m = jax.nn.sigmoid(gp_ref[...]) * bp + jax.nn.sigmoid(ga_ref[...]) * ba
    y_ref[...] = x_ref[...] + jnp.dot(m.astype(BF16), wo_ref[...], preferred_element_type=F32)


def _merge(x, a, o, p, w_bp, w_ba, w_out, gate_block):
    t, d = x.shape
    pw, aw = a.shape[1], o.shape[1]
    tm = min(t, 256)
    const = functools.partial(pl.BlockSpec, pipeline_mode=pl.Buffered(1))
    return pl.pallas_call(
        _merge_kernel,
        grid=(pl.cdiv(t, tm),),
        in_specs=[
            pl.BlockSpec((tm, d), lambda i: (i, 0)),
            pl.BlockSpec((tm, pw), lambda i: (i, 0)),
            pl.BlockSpec((tm, aw), lambda i: (i, 0)),
            pl.BlockSpec((tm, d), lambda i: (i, gate_block)),
            pl.BlockSpec((tm, d), lambda i: (i, gate_block + 1)),
            const((pw, d), lambda i: (0, 0)),
            const((aw, d), lambda i: (0, 0)),
            const((d, d), lambda i: (0, 0)),
        ],
        out_specs=pl.BlockSpec((tm, d), lambda i: (i, 0)),
        out_shape=jax.ShapeDtypeStruct((t, d), F32),
        compiler_params=_cparams(("parallel",)),
        name="merge",
    )(x, a, o, p, p, w_bp, w_ba, w_out)


def kernel(x_prompt, x_sample, cache_k, cache_v, state_pool, page_table, meta_tokens, g_ffn1, w_ffn1_gu, w_ffn1_down, g_mix, w_in, g_q, g_k, sb_bias, w_pool, pool_scale, w_branch_pool, w_branch_attn, w_out, g_ffn2, w_ffn2_gu, w_ffn2_down):
    b, s, d = x_prompt.shape
    db, dt, _ = x_sample.shape
    depth, n_heads = sb_bias.shape
    hd = g_q.shape[1]
    aw = n_heads * hd
    pw = state_pool.shape[-1]
    in_w = w_in.shape[-1]
    n_meta = meta_tokens.shape[0]
    past_len = page_table.shape[1] * PAGE_SIZE
    assert depth == 1, "meta rows are dropped after the mixer; a second layer would need them"
    assert pw == aw and in_w == pw + 3 * aw + 2 * d and n_meta == N_META
    l = 0

    wgu1, wd1 = _prep_ffn_weights(w_ffn1_gu[l], w_ffn1_down[l])
    wgu2, wd2 = _prep_ffn_weights(w_ffn2_gu[l], w_ffn2_down[l])
    w_in_b = w_in[l].astype(BF16)
    w_bp = w_branch_pool[l].astype(BF16)
    w_ba = w_branch_attn[l].astype(BF16)
    w_o = w_out[l].astype(BF16)
    w_pl = w_pool[l].astype(BF16)
    n_chunks = in_w // aw
    head_gain = jnp.ones((n_chunks, 1, aw), F32)
    head_gain = head_gain.at[1, 0].set(jnp.tile(g_q[l], n_heads)).at[2, 0].set(jnp.tile(g_k[l], n_heads))
    gate_block = (pw + 3 * aw) // d

    xr = x_prompt.reshape(b * s, d)
    xe = jnp.concatenate([meta_tokens.astype(F32), x_sample.reshape(db * dt, d)], axis=0)

    x1r = _ffn_half(xr, g_ffn1[l], wgu1, wd1)
    x1e = _ffn_half(xe, g_ffn1[l], wgu1, wd1)
    pr = _in_proj(x1r, g_mix[l], w_in_b, head_gain, n_heads, hd)
    pe = _in_proj(x1e, g_mix[l], w_in_b, head_gain, n_heads, hd)
    pm = pe[:n_meta]
    ps = pe[n_meta:]
    p3 = pr.reshape(b, s, in_w)

    zrow = jnp.zeros((1, pw), F32)
    hist_p = jnp.concatenate([zrow, pm[n_meta - POOL_HIST:, :pw]], axis=0)[None]
    a_p = _pool_mixer(p3, hist_p, w_pl, pool_scale[l], n_meta, pw)
    pm_pad = jnp.pad(pm, ((0, PAGE_SIZE - n_meta), (0, 0)))
    o_p = _prompt_attention(p3, pm_pad, sb_bias[l], n_heads, hd, n_meta)
    x2r = _merge(x1r, a_p.reshape(b * s, pw), o_p.reshape(b * s, aw), pr, w_bp, w_ba, w_o, gate_block)
    y_prompt = _ffn_half(x2r, g_ffn2[l], wgu2, wd2).reshape(b, s, d)

    u_s = ps[:, :pw].reshape(db, dt, pw)
    rows_s = HIST_ROWS * pl.cdiv(dt, HIST_ROWS)
    u_s_pad = jnp.pad(u_s, ((0, 0), (0, rows_s - dt), (0, 0)))
    hist_s = jnp.concatenate([jnp.zeros((db, 1, pw), F32), state_pool[l]], axis=1)
    a_s = _pool_mixer(u_s_pad, hist_s, w_pl, pool_scale[l], past_len, pw)[:, :dt]
    q_s = ps[:, pw:pw + aw].reshape(db, dt, aw)
    k_s = ps[:, pw + aw:pw + 2 * aw].reshape(db, dt, aw)
    v_s = ps[:, pw + 2 * aw:pw + 3 * aw].reshape(db, dt, aw)
    ck = cache_k[l].reshape(cache_k.shape[1], PAGE_SIZE, aw)
    cv = cache_v[l].reshape(cache_v.shape[1], PAGE_SIZE, aw)
    o_s = _sample_attention(q_s, k_s, v_s, ck, cv, page_table, sb_bias[l], n_heads, hd)
    x1s = x1e[n_meta:]
    x2s = _merge(x1s, a_s.reshape(db * dt, pw), o_s.reshape(db * dt, aw).astype(BF16), ps, w_bp, w_ba, w_o, gate_block)
    y_sample = _ffn_half(x2s, g_ffn2[l], wgu2, wd2).reshape(db, dt, d)

    def with_meta(col0):
        meta = jnp.broadcast_to(pm[None, :, col0:col0 + aw], (b, n_meta, aw))
        return jnp.concatenate([meta, p3[:, :, col0:col0 + aw]], axis=1).reshape(1, b, s + n_meta, n_heads, hd)

    k_prompt = with_meta(pw + aw)
    v_prompt = with_meta(pw + 2 * aw)
    pool_prompt = p3[:, s - POOL_HIST:, :pw][None]
    pool_sample = jnp.concatenate([state_pool[l], u_s], axis=1)[:, -POOL_HIST:][None]
    return (y_prompt, y_sample, k_prompt, v_prompt, pool_prompt,
            k_s.reshape(1, db, dt, n_heads, hd), v_s.reshape(1, db, dt, n_heads, hd), pool_sample)
```

```python
import functools

import jax
import jax.numpy as jnp
from jax import lax
from jax.experimental import pallas as pl
from jax.experimental.pallas import tpu as pltpu

F32 = jnp.float32
BF16 = jnp.bfloat16

EPS = 1e-6
N_META = 16
POOL_WINDOWS = (2, 4, 8, 16)
POOL_HIST = max(POOL_WINDOWS) - 1
HIST_ROWS = 16
PAGE_SIZE = 128
PAGES_PER_STEP = 8
FF_CHUNK = 512
ROW_TILE = 512
FFN_ROW_TILE = 1024
ATTN_TILE = 256
VMEM_LIMIT = 60 * 1024 * 1024


def _cparams(sem):
    return pltpu.CompilerParams(dimension_semantics=sem, vmem_limit_bytes=VMEM_LIMIT)


def _row_tile(t):
    return t if t <= ROW_TILE else ROW_TILE


def _rms_rows(x, g):
    ms = jnp.mean(x * x, axis=-1, keepdims=True)
    return x * lax.rsqrt(ms + EPS) * g


def _ffn_kernel(x_ref, g_ref, wg_ref, wu_ref, wd_ref, o_ref, xn_ref, *, n_col):
    j = pl.program_id(1)

    @pl.when(j == 0)
    def _():
        x = x_ref[...]
        xn_ref[...] = _rms_rows(x, g_ref[...]).astype(BF16)
        o_ref[...] = x

    xn = xn_ref[...]
    a = jnp.dot(xn, wg_ref[...], preferred_element_type=F32)
    b = jnp.dot(xn, wu_ref[...], preferred_element_type=F32)
    act = (((0.5 * a) * jax.nn.sigmoid(a)) * b).astype(BF16)
    cw = o_ref.shape[1] // n_col
    for n in range(n_col):
        cols = slice(n * cw, (n + 1) * cw)
        o_ref[:, cols] += jnp.dot(act, wd_ref[:, cols], preferred_element_type=F32)


def _ffn_half(x, g, wg, wu, wd):
    t, d = x.shape
    tf = FF_CHUNK
    n_ff = wd.shape[0] // tf
    tm = t if t <= FFN_ROW_TILE else FFN_ROW_TILE
    return pl.pallas_call(
        functools.partial(_ffn_kernel, n_col=4),
        grid=(pl.cdiv(t, tm), n_ff),
        in_specs=[
            pl.BlockSpec((tm, d), lambda i, j: (i, 0)),
            pl.BlockSpec((1, d), lambda i, j: (0, 0)),
            pl.BlockSpec((d, tf), lambda i, j: (0, j)),
            pl.BlockSpec((d, tf), lambda i, j: (0, j)),
            pl.BlockSpec((tf, d), lambda i, j: (j, 0)),
        ],
        out_specs=pl.BlockSpec((tm, d), lambda i, j: (i, 0)),
        out_shape=jax.ShapeDtypeStruct((t, d), F32),
        scratch_shapes=[pltpu.VMEM((tm, d), BF16)],
        compiler_params=_cparams(("parallel", "arbitrary")),
        name="ffn_half",
    )(x, g.reshape(1, d), wg, wu, wd)


def _prep_ffn_weights(w_gu, w_down):
    f = w_gu.shape[1] // 2
    pad = FF_CHUNK * pl.cdiv(f, FF_CHUNK) - f
    wg = jnp.pad(w_gu[:, :f].astype(BF16), ((0, 0), (0, pad)))
    wu = jnp.pad(w_gu[:, f:].astype(BF16), ((0, 0), (0, pad)))
    wd = jnp.pad(w_down.astype(BF16), ((0, pad), (0, 0)))
    return wg, wu, wd


def _inproj_kernel(x_ref, g_ref, w_ref, hg_ref, o_ref, xn_ref, *, n_heads, hd):
    j = pl.program_id(1)

    @pl.when(j == 0)
    def _():
        xn_ref[...] = _rms_rows(x_ref[...], g_ref[...]).astype(BF16)

    p = jnp.dot(xn_ref[...], w_ref[...], preferred_element_type=F32)
    is_qk = jnp.logical_or(j == 1, j == 2)

    @pl.when(is_qk)
    def _():
        for h in range(n_heads):
            cols = slice(h * hd, (h + 1) * hd)
            o_ref[:, cols] = _rms_rows(p[:, cols], hg_ref[:, cols])

    @pl.when(jnp.logical_not(is_qk))
    def _():
        o_ref[...] = p


def _in_proj(x, g_mix, w_in, head_gain, n_heads, hd):
    t, d = x.shape
    in_w = w_in.shape[1]
    tn = n_heads * hd
    tm = t if t <= FFN_ROW_TILE else FFN_ROW_TILE
    return pl.pallas_call(
        functools.partial(_inproj_kernel, n_heads=n_heads, hd=hd),
        grid=(pl.cdiv(t, tm), in_w // tn),
        in_specs=[
            pl.BlockSpec((tm, d), lambda i, j: (i, 0)),
            pl.BlockSpec((1, d), lambda i, j: (0, 0)),
            pl.BlockSpec((d, tn), lambda i, j: (0, j)),
            pl.BlockSpec((None, 1, tn), lambda i, j: (j, 0, 0)),
        ],
        out_specs=pl.BlockSpec((tm, tn), lambda i, j: (i, j)),
        out_shape=jax.ShapeDtypeStruct((t, in_w), F32),
        scratch_shapes=[pltpu.VMEM((tm, d), BF16)],
        compiler_params=_cparams(("parallel", "arbitrary")),
        name="in_proj",
    )(x, g_mix.reshape(1, d), w_in, head_gain)


def _pool_kernel(u_ref, halo_ref, hist_ref, wp_ref, ps_ref, o_ref, *, tm, pos0, group):
    i = pl.program_id(1)
    prev = jnp.where(i == 0, hist_ref[...], halo_ref[...])
    pos = pos0 + i * tm + lax.broadcasted_iota(jnp.int32, (tm, 1), 0)
    for gi, w in enumerate(POOL_WINDOWS):
        cols = slice(gi * group, (gi + 1) * group)
        x = jnp.concatenate([prev[:, cols], u_ref[:, cols]], axis=0)
        s = x
        span = 1
        while span < w:
            s = s + pltpu.roll(s, span, axis=0)
            span *= 2
        count = jnp.minimum(w, pos + 1).astype(F32)
        d = s[HIST_ROWS:] / count - x[HIST_ROWS:]
        y = jnp.dot(d.astype(BF16), wp_ref[gi], preferred_element_type=F32)
        o_ref[:, cols] = (y * ps_ref[:, cols]).astype(o_ref.dtype)


def _pool_mixer(p3, hist, w_pool, pool_scale, pos0, pw):
    b, s, _ = p3.shape
    tm = s if s <= ROW_TILE else ROW_TILE
    n_groups = w_pool.shape[0]
    group = pw // n_groups
    halo_blocks = tm // HIST_ROWS
    per_batch = hist.shape[0] != 1
    return pl.pallas_call(
        functools.partial(_pool_kernel, tm=tm, pos0=pos0, group=group),
        grid=(b, s // tm),
        in_specs=[
            pl.BlockSpec((None, tm, pw), lambda bi, i: (bi, i, 0)),
            pl.BlockSpec((None, HIST_ROWS, pw), lambda bi, i: (bi, jnp.maximum(i * halo_blocks - 1, 0), 0)),
            pl.BlockSpec((None, HIST_ROWS, pw), (lambda bi, i: (bi, 0, 0)) if per_batch else (lambda bi, i: (0, 0, 0))),
            pl.BlockSpec((n_groups, group, group), lambda bi, i: (0, 0, 0)),
            pl.BlockSpec((1, pw), lambda bi, i: (0, 0)),
        ],
        out_specs=pl.BlockSpec((None, tm, pw), lambda bi, i: (bi, i, 0)),
        out_shape=jax.ShapeDtypeStruct((b, s, pw), BF16),
        compiler_params=_cparams(("parallel", "arbitrary")),
        name="pool_mixer",
    )(p3, p3, hist, w_pool, pool_scale.reshape(1, pw))


LOG2E = 1.4426950408889634


def _sb_block(qb, kb, vb, tri, bias2, mask):
    z = lax.dot_general(qb, kb, (((1,), (1,)), ((), ())), preferred_element_type=F32) + bias2
    nls = jnp.maximum(z, 0.0) + jnp.log2(1.0 + jnp.exp2(-jnp.abs(z)))
    nls_vis = nls if mask is None else jnp.where(mask, nls, 0.0)
    later = jnp.dot(nls_vis.astype(BF16), tri, preferred_element_type=F32)
    w = jnp.exp2(z - nls - later)
    if mask is not None:
        w = jnp.where(mask, w, 0.0)
    pv = jnp.dot(w.astype(BF16), vb, preferred_element_type=F32)
    csum = later[:, :1] + nls_vis[:, :1]
    return csum, pv


def _sb_chunk(qb, kc, vc, tri, bias2):
    m = qb.shape[0]
    tb = tri.shape[0]
    n_sub = kc.shape[0] // tb
    z = lax.dot_general(qb, kc, (((1,), (1,)), ((), ())), preferred_element_type=F32) + bias2
    nls = jnp.maximum(z, 0.0) + jnp.log2(1.0 + jnp.exp2(-jnp.abs(z)))
    nls_rows = jnp.concatenate([nls[:, j * tb:(j + 1) * tb] for j in range(n_sub)], axis=0)
    later_rows = jnp.dot(nls_rows.astype(BF16), tri, preferred_element_type=F32)
    sub_sum = later_rows[:, :1] + nls_rows[:, :1]
    parts = []
    tail = jnp.zeros((m, 1), F32)
    for j in reversed(range(n_sub)):
        parts.append(later_rows[j * m:(j + 1) * m] + tail)
        tail = tail + sub_sum[j * m:(j + 1) * m]
    later = jnp.concatenate(parts[::-1], axis=1)
    w = jnp.exp2(z - nls - later)
    pv = jnp.dot(w.astype(BF16), vc, preferred_element_type=F32)
    return tail, pv


def _sb_fold(c, acc, csum, pv):
    return c + csum, acc + jnp.exp2(-c) * pv


def _pattn_kernel(bias_ref, q_ref, k_ref, v_ref, km_ref, vm_ref, tri_ref, o_ref,
                  kb_ref, vb_ref, kmb_ref, vmb_ref, *, tk, hd, n_meta):
    h = pl.program_id(1)
    i = pl.program_id(2)
    tq = 2 * tk
    bias2 = bias_ref[h] * LOG2E

    @pl.when(i == 0)
    def _():
        kb_ref[...] = k_ref[...].astype(BF16)
        vb_ref[...] = v_ref[...].astype(BF16)
        kmb_ref[...] = km_ref[...].astype(BF16)
        vmb_ref[...] = vm_ref[...].astype(BF16)

    qb = (q_ref[...] * (hd ** -0.5 * LOG2E)).astype(BF16)
    tri = tri_ref[...]

    def kv(j):
        st = pl.multiple_of(j * tk, tk)
        return kb_ref[pl.ds(st, tk), :], vb_ref[pl.ds(st, tk), :]

    k0, v0 = kv(2 * i)
    k1, v1 = kv(2 * i + 1)
    lower = lax.broadcasted_iota(jnp.int32, (tk, tk), 1) < lax.broadcasted_iota(jnp.int32, (tk, tk), 0)
    cs1, pv1 = _sb_block(qb[tk:], k1, v1, tri, bias2, lower)
    c = jnp.concatenate([jnp.zeros((tk, 1), F32), cs1], axis=0)
    acc = jnp.concatenate([jnp.zeros((tk, hd), F32), pv1], axis=0)
    vis0 = lax.broadcasted_iota(jnp.int32, (tq, tk), 1) < lax.broadcasted_iota(jnp.int32, (tq, tk), 0)
    c, acc = _sb_fold(c, acc, *_sb_block(qb, k0, v0, tri, bias2, vis0))

    def body(t, carry):
        c, acc = carry
        ka, va = kv(2 * i - 1 - 2 * t)
        kb, vb = kv(2 * i - 2 - 2 * t)
        blk_a = _sb_block(qb, ka, va, tri, bias2, None)
        blk_b = _sb_block(qb, kb, vb, tri, bias2, None)
        c, acc = _sb_fold(c, acc, *blk_a)
        return _sb_fold(c, acc, *blk_b)

    c, acc = lax.fori_loop(0, i, body, (c, acc))

    nm = kmb_ref.shape[0]
    mcol = lax.broadcasted_iota(jnp.int32, (tq, nm), 1)
    c, acc = _sb_fold(c, acc, *_sb_block(qb, kmb_ref[...], vmb_ref[...], tri[:nm, :nm], bias2, mcol < n_meta))
    o_ref[...] = acc.astype(o_ref.dtype)


def _prompt_attention(p3, pm_pad, sb_bias, n_heads, hd, n_meta):
    b, s, _ = p3.shape
    aw = n_heads * hd
    tk = ATTN_TILE
    tq = 2 * tk
    assert s % tq == 0
    nm = pm_pad.shape[0]
    qo, ko, vo = aw // hd, 2 * aw // hd, 3 * aw // hd
    tri = (lax.broadcasted_iota(jnp.int32, (tk, tk), 0) > lax.broadcasted_iota(jnp.int32, (tk, tk), 1)).astype(BF16)
    return pl.pallas_call(
        functools.partial(_pattn_kernel, tk=tk, hd=hd, n_meta=n_meta),
        grid=(b, n_heads, s // tq),
        in_specs=[
            pl.BlockSpec(memory_space=pltpu.SMEM),
            pl.BlockSpec((None, tq, hd), lambda bi, h, i: (bi, i, qo + h)),
            pl.BlockSpec((None, s, hd), lambda bi, h, i: (bi, 0, ko + h)),
            pl.BlockSpec((None, s, hd), lambda bi, h, i: (bi, 0, vo + h)),
            pl.BlockSpec((nm, hd), lambda bi, h, i: (0, ko + h)),
            pl.BlockSpec((nm, hd), lambda bi, h, i: (0, vo + h)),
            pl.BlockSpec((tk, tk), lambda bi, h, i: (0, 0)),
        ],
        out_specs=pl.BlockSpec((None, tq, hd), lambda bi, h, i: (bi, i, h)),
        out_shape=jax.ShapeDtypeStruct((b, s, aw), BF16),
        scratch_shapes=[pltpu.VMEM((s, hd), BF16), pltpu.VMEM((s, hd), BF16),
                        pltpu.VMEM((nm, hd), BF16), pltpu.VMEM((nm, hd), BF16)],
        compiler_params=_cparams(("parallel", "arbitrary", "arbitrary")),
        name="prompt_attn",
    )(sb_bias, p3, p3, p3, pm_pad, pm_pad, tri)


def _sattn_kernel(pt_ref, qbd_ref, bias_ref, kn_ref, vn_ref, tri_ref, *rest, hd, n_heads, dt, n_steps):
    page_refs = rest[:2 * PAGES_PER_STEP]
    o_ref, c_ref, acc_ref = rest[2 * PAGES_PER_STEP:]
    s = pl.program_id(1)
    qb = qbd_ref[...]
    bias2 = bias_ref[...]
    tri = tri_ref[...]
    m = qb.shape[0]

    @pl.when(s == 0)
    def _():
        n_new = kn_ref.shape[0]
        zpad = jnp.zeros((PAGE_SIZE - n_new, kn_ref.shape[1]), BF16)
        kn = jnp.concatenate([kn_ref[...].astype(BF16), zpad], axis=0)
        vn = jnp.concatenate([vn_ref[...].astype(BF16), zpad], axis=0)
        qi = lax.rem(lax.broadcasted_iota(jnp.int32, (m, PAGE_SIZE), 0), dt)
        col = lax.broadcasted_iota(jnp.int32, (m, PAGE_SIZE), 1)
        c_ref[...], acc_ref[...] = _sb_block(qb, kn, vn, tri[:PAGE_SIZE, :PAGE_SIZE], bias2, col < qi)

    kc = jnp.concatenate([page_refs[r][...].astype(BF16) for r in range(PAGES_PER_STEP)], axis=0)
    vc = jnp.concatenate([page_refs[PAGES_PER_STEP + r][...].astype(BF16) for r in range(PAGES_PER_STEP)], axis=0)
    c_ref[...], acc_ref[...] = _sb_fold(c_ref[...], acc_ref[...], *_sb_chunk(qb, kc, vc, tri, bias2))

    @pl.when(s == n_steps - 1)
    def _():
        for h in range(n_heads):
            o_ref[h * dt:(h + 1) * dt, :] = acc_ref[h * dt:(h + 1) * dt, h * hd:(h + 1) * hd]


def _sample_attention(q, k_new, v_new, cache_k, cache_v, page_table, page0, sb_bias, n_heads, hd):
    db, dt, aw = q.shape
    n_pages = page_table.shape[1]
    assert n_pages % PAGES_PER_STEP == 0
    n_steps = n_pages // PAGES_PER_STEP
    m = n_heads * dt
    q4 = (q * (hd ** -0.5 * LOG2E)).reshape(db, dt, n_heads, hd).transpose(0, 2, 1, 3)
    eye = jnp.eye(n_heads, dtype=F32)
    qbd = (q4[:, :, :, None, :] * eye[None, :, None, :, None]).reshape(db, m, aw).astype(BF16)
    bias_rows = jnp.repeat(sb_bias * LOG2E, dt).reshape(m, 1)
    new_rows = 8 * pl.cdiv(dt, 8)
    kn = jnp.pad(k_new, ((0, 0), (0, new_rows - dt), (0, 0)))
    vn = jnp.pad(v_new, ((0, 0), (0, new_rows - dt), (0, 0)))
    tb = 2 * PAGE_SIZE
    tri = (lax.broadcasted_iota(jnp.int32, (tb, tb), 0) > lax.broadcasted_iota(jnp.int32, (tb, tb), 1)).astype(BF16)

    def page_spec(r):
        def index(bi, s, pt):
            return (page0 + pt[bi * n_pages + (n_pages - (s + 1) * PAGES_PER_STEP + r)], 0, 0)
        return pl.BlockSpec((None, PAGE_SIZE, aw), index)

    grid_spec = pltpu.PrefetchScalarGridSpec(
        num_scalar_prefetch=1,
        grid=(db, n_steps),
        in_specs=[
            pl.BlockSpec((None, m, aw), lambda bi, s, pt: (bi, 0, 0)),
            pl.BlockSpec((m, 1), lambda bi, s, pt: (0, 0)),
            pl.BlockSpec((None, new_rows, aw), lambda bi, s, pt: (bi, 0, 0)),
            pl.BlockSpec((None, new_rows, aw), lambda bi, s, pt: (bi, 0, 0)),
            pl.BlockSpec((tb, tb), lambda bi, s, pt: (0, 0)),
        ] + [page_spec(r) for r in range(PAGES_PER_STEP)] * 2,
        out_specs=pl.BlockSpec((None, m, hd), lambda bi, s, pt: (bi, 0, 0)),
        scratch_shapes=[pltpu.VMEM((m, 1), F32), pltpu.VMEM((m, aw), F32)],
    )
    out = pl.pallas_call(
        functools.partial(_sattn_kernel, hd=hd, n_heads=n_heads, dt=dt, n_steps=n_steps),
        grid_spec=grid_spec,
        out_shape=jax.ShapeDtypeStruct((db, m, hd), F32),
        compiler_params=_cparams(("parallel", "arbitrary")),
        name="sample_attn",
    )(page_table.reshape(-1), qbd, bias_rows, kn, vn, tri,
      *([cache_k] * PAGES_PER_STEP), *([cache_v] * PAGES_PER_STEP))
    return out.reshape(db, n_heads, dt, hd).transpose(0, 2, 1, 3).reshape(db, dt, aw)


def _merge_kernel(x_ref, a_ref, o_ref, gp_ref, ga_ref, wbp_ref, wba_ref, wo_ref, y_ref):
    bp = jnp.dot(a_ref[...].astype(BF16), wbp_ref[...], preferred_element_type=F32)
    ba = jnp.dot(o_ref[...].astype(BF16), wba_ref[...], preferred_element_type=F32)
    m = jax.nn.sigmoid(gp_ref[...]) * bp + jax.nn.sigmoid(ga_ref[...]) * ba
    y_ref[...] = x_ref[...] + jnp.dot(m.astype(BF16), wo_ref[...], preferred_element_type=F32)


def _merge(x, a, o, p, w_bp, w_ba, w_out, gate_block):
    t, d = x.shape
    pw, aw = a.shape[1], o.shape[1]
    tm = min(t, 256)
    const = functools.partial(pl.BlockSpec, pipeline_mode=pl.Buffered(1))
    return pl.pallas_call(
        _merge_kernel,
        grid=(pl.cdiv(t, tm),),
        in_specs=[
            pl.BlockSpec((tm, d), lambda i: (i, 0)),
            pl.BlockSpec((tm, pw), lambda i: (i, 0)),
            pl.BlockSpec((tm, aw), lambda i: (i, 0)),
            pl.BlockSpec((tm, d), lambda i: (i, gate_block)),
            pl.BlockSpec((tm, d), lambda i: (i, gate_block + 1)),
            const((pw, d), lambda i: (0, 0)),
            const((aw, d), lambda i: (0, 0)),
            const((d, d), lambda i: (0, 0)),
        ],
        out_specs=pl.BlockSpec((tm, d), lambda i: (i, 0)),
        out_shape=jax.ShapeDtypeStruct((t, d), F32),
        compiler_params=_cparams(("parallel",)),
        name="merge",
    )(x, a, o, p, p, w_bp, w_ba, w_out)


def kernel(x_prompt, x_sample, cache_k, cache_v, state_pool, page_table, meta_tokens, g_ffn1, w_ffn1_gu, w_ffn1_down, g_mix, w_in, g_q, g_k, sb_bias, w_pool, pool_scale, w_branch_pool, w_branch_attn, w_out, g_ffn2, w_ffn2_gu, w_ffn2_down):
    b, s, d = x_prompt.shape
    db, dt, _ = x_sample.shape
    depth, n_heads = sb_bias.shape
    hd = g_q.shape[1]
    aw = n_heads * hd
    pw = state_pool.shape[-1]
    in_w = w_in.shape[-1]
    n_meta = meta_tokens.shape[0]
    past_len = page_table.shape[1] * PAGE_SIZE
    assert depth == 1, "meta rows are dropped after the mixer; a second layer would need them"
    assert pw == aw and in_w == pw + 3 * aw + 2 * d and n_meta == N_META
    l = 0

    ffn1 = _prep_ffn_weights(w_ffn1_gu[l], w_ffn1_down[l])
    ffn2 = _prep_ffn_weights(w_ffn2_gu[l], w_ffn2_down[l])
    w_in_b = w_in[l].astype(BF16)
    w_bp = w_branch_pool[l].astype(BF16)
    w_ba = w_branch_attn[l].astype(BF16)
    w_o = w_out[l].astype(BF16)
    w_pl = w_pool[l].astype(BF16)
    n_chunks = in_w // aw
    head_gain = jnp.ones((n_chunks, 1, aw), F32)
    head_gain = head_gain.at[1, 0].set(jnp.tile(g_q[l], n_heads)).at[2, 0].set(jnp.tile(g_k[l], n_heads))
    gate_block = (pw + 3 * aw) // d

    xr = x_prompt.reshape(b * s, d)
    xe = jnp.concatenate([meta_tokens.astype(F32), x_sample.reshape(db * dt, d)], axis=0)

    x1r = _ffn_half(xr, g_ffn1[l], *ffn1)
    x1e = _ffn_half(xe, g_ffn1[l], *ffn1)
    pr = _in_proj(x1r, g_mix[l], w_in_b, head_gain, n_heads, hd)
    pe = _in_proj(x1e, g_mix[l], w_in_b, head_gain, n_heads, hd)
    pm = pe[:n_meta]
    ps = pe[n_meta:]
    p3 = pr.reshape(b, s, in_w)

    zrow = jnp.zeros((1, pw), F32)
    hist_p = jnp.concatenate([zrow, pm[n_meta - POOL_HIST:, :pw]], axis=0)[None]
    a_p = _pool_mixer(p3, hist_p, w_pl, pool_scale[l], n_meta, pw)
    pm_pad = jnp.pad(pm, ((0, PAGE_SIZE - n_meta), (0, 0)))
    o_p = _prompt_attention(p3, pm_pad, sb_bias[l], n_heads, hd, n_meta)
    x2r = _merge(x1r, a_p.reshape(b * s, pw), o_p.reshape(b * s, aw), pr, w_bp, w_ba, w_o, gate_block)
    y_prompt = _ffn_half(x2r, g_ffn2[l], *ffn2).reshape(b, s, d)

    u_s = ps[:, :pw].reshape(db, dt, pw)
    rows_s = HIST_ROWS * pl.cdiv(dt, HIST_ROWS)
    u_s_pad = jnp.pad(u_s, ((0, 0), (0, rows_s - dt), (0, 0)))
    hist_s = jnp.concatenate([jnp.zeros((db, 1, pw), F32), state_pool[l]], axis=1)
    a_s = _pool_mixer(u_s_pad, hist_s, w_pl, pool_scale[l], past_len, pw)[:, :dt]
    q_s = ps[:, pw:pw + aw].reshape(db, dt, aw)
    k_s = ps[:, pw + aw:pw + 2 * aw].reshape(db, dt, aw)
    v_s = ps[:, pw + 2 * aw:pw + 3 * aw].reshape(db, dt, aw)
    n_pool = cache_k.shape[1]
    ck = cache_k.reshape(depth * n_pool, PAGE_SIZE, aw)
    cv = cache_v.reshape(depth * n_pool, PAGE_SIZE, aw)
    o_s = _sample_attention(q_s, k_s, v_s, ck, cv, page_table, l * n_pool, sb_bias[l], n_heads, hd)
    x1s = x1e[n_meta:]
    x2s = _merge(x1s, a_s.reshape(db * dt, pw), o_s.reshape(db * dt, aw).astype(BF16), ps, w_bp, w_ba, w_o, gate_block)
    y_sample = _ffn_half(x2s, g_ffn2[l], *ffn2).reshape(db, dt, d)

    def with_meta(col0):
        meta = jnp.broadcast_to(pm[None, :, col0:col0 + aw], (b, n_meta, aw))
        return jnp.concatenate([meta, p3[:, :, col0:col0 + aw]], axis=1).reshape(1, b, s + n_meta, n_heads, hd)

    k_prompt = with_meta(pw + aw)
    v_prompt = with_meta(pw + 2 * aw)
    pool_prompt = p3[:, s - POOL_HIST:, :pw][None]
    pool_sample = jnp.concatenate([state_pool[l], u_s], axis=1)[:, -POOL_HIST:][None]
    return (y_prompt, y_sample, k_prompt, v_prompt, pool_prompt,
            k_s.reshape(1, db, dt, n_heads, hd), v_s.reshape(1, db, dt, n_heads, hd), pool_sample)
```

```python
import functools

import jax
import jax.numpy as jnp
from jax import lax
from jax.experimental import pallas as pl
from jax.experimental.pallas import tpu as pltpu

F32 = jnp.float32
BF16 = jnp.bfloat16

EPS = 1e-6
N_META = 16
POOL_WINDOWS = (2, 4, 8, 16)
POOL_HIST = max(POOL_WINDOWS) - 1
HIST_ROWS = 16
PAGE_SIZE = 128
PAGES_PER_STEP = 8
FF_CHUNK = 512
ROW_TILE = 512
FFN_ROW_TILE = 1024
ATTN_TILE = 256
VMEM_LIMIT = 60 * 1024 * 1024


def _cparams(sem):
    return pltpu.CompilerParams(dimension_semantics=sem, vmem_limit_bytes=VMEM_LIMIT)


def _row_tile(t):
    return t if t <= ROW_TILE else ROW_TILE


def _rms_rows(x, g):
    ms = jnp.mean(x * x, axis=-1, keepdims=True)
    return x * lax.rsqrt(ms + EPS) * g


def _ffn_kernel(x_ref, g_ref, wg_ref, wu_ref, wd_ref, o_ref, xn_ref, *, n_col):
    j = pl.program_id(1)

    @pl.when(j == 0)
    def _():
        x = x_ref[...]
        xn_ref[...] = _rms_rows(x, g_ref[...]).astype(BF16)
        o_ref[...] = x

    xn = xn_ref[...]
    a = jnp.dot(xn, wg_ref[...], preferred_element_type=F32)
    b = jnp.dot(xn, wu_ref[...], preferred_element_type=F32)
    act = (((0.5 * a) * jax.nn.sigmoid(a)) * b).astype(BF16)
    cw = o_ref.shape[1] // n_col
    for n in range(n_col):
        cols = slice(n * cw, (n + 1) * cw)
        o_ref[:, cols] += jnp.dot(act, wd_ref[:, cols], preferred_element_type=F32)


def _ffn_half(x, g, wg, wu, wd):
    t, d = x.shape
    tf = FF_CHUNK
    n_ff = wd.shape[0] // tf
    tm = t if t <= FFN_ROW_TILE else FFN_ROW_TILE
    return pl.pallas_call(
        functools.partial(_ffn_kernel, n_col=4),
        grid=(pl.cdiv(t, tm), n_ff),
        in_specs=[
            pl.BlockSpec((tm, d), lambda i, j: (i, 0)),
            pl.BlockSpec((1, d), lambda i, j: (0, 0)),
            pl.BlockSpec((d, tf), lambda i, j: (0, j)),
            pl.BlockSpec((d, tf), lambda i, j: (0, j)),
            pl.BlockSpec((tf, d), lambda i, j: (j, 0)),
        ],
        out_specs=pl.BlockSpec((tm, d), lambda i, j: (i, 0)),
        out_shape=jax.ShapeDtypeStruct((t, d), F32),
        scratch_shapes=[pltpu.VMEM((tm, d), BF16)],
        compiler_params=_cparams(("parallel", "arbitrary")),
        name="ffn_half",
    )(x, g.reshape(1, d), wg, wu, wd)


def _prep_ffn_weights(w_gu, w_down):
    f = w_gu.shape[1] // 2
    pad = FF_CHUNK * pl.cdiv(f, FF_CHUNK) - f
    wg = jnp.pad(w_gu[:, :f].astype(BF16), ((0, 0), (0, pad)))
    wu = jnp.pad(w_gu[:, f:].astype(BF16), ((0, 0), (0, pad)))
    wd = jnp.pad(w_down.astype(BF16), ((0, pad), (0, 0)))
    return wg, wu, wd


def _inproj_kernel(x_ref, g_ref, w_ref, hg_ref, p_ref, k_ref, v_ref, xn_ref, *, n_heads, hd):
    j = pl.program_id(1)

    @pl.when(j == 0)
    def _():
        xn_ref[...] = _rms_rows(x_ref[...], g_ref[...]).astype(BF16)

    p = jnp.dot(xn_ref[...], w_ref[...], preferred_element_type=F32)
    heads = [slice(h * hd, (h + 1) * hd) for h in range(n_heads)]

    @pl.when(j == 1)
    def _():
        for cols in heads:
            p_ref[:, cols] = _rms_rows(p[:, cols], hg_ref[:, cols])

    @pl.when(j == 2)
    def _():
        for h, cols in enumerate(heads):
            kh = _rms_rows(p[:, cols], hg_ref[:, cols])
            p_ref[:, cols] = kh
            k_ref[:, h, :] = kh

    @pl.when(j == 3)
    def _():
        p_ref[...] = p
        for h, cols in enumerate(heads):
            v_ref[:, h, :] = p[:, cols]

    @pl.when(jnp.logical_or(j == 0, j > 3))
    def _():
        p_ref[...] = p


def _in_proj(x, g_mix, w_in, head_gain, n_heads, hd):
    t, d = x.shape
    in_w = w_in.shape[1]
    tn = n_heads * hd
    tm = t if t <= FFN_ROW_TILE else FFN_ROW_TILE
    kv_spec = pl.BlockSpec((tm, n_heads, hd), lambda i, j: (i, 0, 0))
    kv_shape = jax.ShapeDtypeStruct((t, n_heads, hd), F32)
    return pl.pallas_call(
        functools.partial(_inproj_kernel, n_heads=n_heads, hd=hd),
        grid=(pl.cdiv(t, tm), in_w // tn),
        in_specs=[
            pl.BlockSpec((tm, d), lambda i, j: (i, 0)),
            pl.BlockSpec((1, d), lambda i, j: (0, 0)),
            pl.BlockSpec((d, tn), lambda i, j: (0, j)),
            pl.BlockSpec((None, 1, tn), lambda i, j: (j, 0, 0)),
        ],
        out_specs=[pl.BlockSpec((tm, tn), lambda i, j: (i, j)), kv_spec, kv_spec],
        out_shape=[jax.ShapeDtypeStruct((t, in_w), F32), kv_shape, kv_shape],
        scratch_shapes=[pltpu.VMEM((tm, d), BF16)],
        compiler_params=_cparams(("parallel", "arbitrary")),
        name="in_proj",
    )(x, g_mix.reshape(1, d), w_in, head_gain)


def _pool_kernel(u_ref, halo_ref, hist_ref, wp_ref, ps_ref, o_ref, *, tm, pos0, group):
    i = pl.program_id(1)
    prev = jnp.where(i == 0, hist_ref[...], halo_ref[...])
    pos = pos0 + i * tm + lax.broadcasted_iota(jnp.int32, (tm, 1), 0)
    for gi, w in enumerate(POOL_WINDOWS):
        cols = slice(gi * group, (gi + 1) * group)
        x = jnp.concatenate([prev[:, cols], u_ref[:, cols]], axis=0)
        s = x
        span = 1
        while span < w:
            s = s + pltpu.roll(s, span, axis=0)
            span *= 2
        count = jnp.minimum(w, pos + 1).astype(F32)
        d = s[HIST_ROWS:] / count - x[HIST_ROWS:]
        y = jnp.dot(d.astype(BF16), wp_ref[gi], preferred_element_type=F32)
        o_ref[:, cols] = (y * ps_ref[:, cols]).astype(o_ref.dtype)


def _pool_mixer(p3, hist, w_pool, pool_scale, pos0, pw):
    b, s, _ = p3.shape
    tm = s if s <= ROW_TILE else ROW_TILE
    n_groups = w_pool.shape[0]
    group = pw // n_groups
    halo_blocks = tm // HIST_ROWS
    per_batch = hist.shape[0] != 1
    return pl.pallas_call(
        functools.partial(_pool_kernel, tm=tm, pos0=pos0, group=group),
        grid=(b, s // tm),
        in_specs=[
            pl.BlockSpec((None, tm, pw), lambda bi, i: (bi, i, 0)),
            pl.BlockSpec((None, HIST_ROWS, pw), lambda bi, i: (bi, jnp.maximum(i * halo_blocks - 1, 0), 0)),
            pl.BlockSpec((None, HIST_ROWS, pw), (lambda bi, i: (bi, 0, 0)) if per_batch else (lambda bi, i: (0, 0, 0))),
            pl.BlockSpec((n_groups, group, group), lambda bi, i: (0, 0, 0)),
            pl.BlockSpec((1, pw), lambda bi, i: (0, 0)),
        ],
        out_specs=pl.BlockSpec((None, tm, pw), lambda bi, i: (bi, i, 0)),
        out_shape=jax.ShapeDtypeStruct((b, s, pw), BF16),
        compiler_params=_cparams(("parallel", "arbitrary")),
        name="pool_mixer",
    )(p3, p3, hist, w_pool, pool_scale.reshape(1, pw))


LOG2E = 1.4426950408889634
MASKED_LOGIT = -1e30


def _sb_block(qb, kb, vb, tri, bias2, mask):
    z = lax.dot_general(qb, kb, (((1,), (1,)), ((), ())), preferred_element_type=F32) + bias2
    nls = jnp.maximum(z, 0.0) + jnp.log2(1.0 + jnp.exp2(-jnp.abs(z)))
    nls_vis = nls if mask is None else jnp.where(mask, nls, 0.0)
    later = jnp.dot(nls_vis.astype(BF16), tri, preferred_element_type=F32)
    w = jnp.exp2(z - nls - later)
    if mask is not None:
        w = jnp.where(mask, w, 0.0)
    pv = jnp.dot(w.astype(BF16), vb, preferred_element_type=F32)
    csum = later[:, :1] + nls_vis[:, :1]
    return csum, pv


def _sb_chunk(qb, kc, vc, tri, bias_tile):
    m = qb.shape[0]
    tb = tri.shape[0]
    n_sub = kc.shape[0] // tb
    z = lax.dot_general(qb, kc, (((1,), (1,)), ((), ())), preferred_element_type=F32)
    z = z + jnp.concatenate([bias_tile] * n_sub, axis=1)
    nls = jnp.maximum(z, 0.0) + jnp.log2(1.0 + jnp.exp2(-jnp.abs(z)))
    nls_rows = jnp.concatenate([nls[:, j * tb:(j + 1) * tb] for j in range(n_sub)], axis=0)
    later_rows = jnp.dot(nls_rows.astype(BF16), tri, preferred_element_type=F32)
    sub_sum = later_rows[:, :1] + nls_rows[:, :1]
    parts = []
    tail = jnp.zeros((m, 1), F32)
    for j in reversed(range(n_sub)):
        parts.append(later_rows[j * m:(j + 1) * m] + tail)
        tail = tail + sub_sum[j * m:(j + 1) * m]
    later = jnp.concatenate(parts[::-1], axis=1)
    w = jnp.exp2(z - nls - later)
    pv = jnp.dot(w.astype(BF16), vc, preferred_element_type=F32)
    return tail, pv


def _sb_fold(c, acc, csum, pv):
    return c + csum, acc + jnp.exp2(-c) * pv


def _pattn_kernel(bias_ref, q_ref, k_ref, v_ref, km_ref, vm_ref, tri_ref, o_ref,
                  kb_ref, vb_ref, kmb_ref, vmb_ref, *, tk, hd, n_meta):
    h = pl.program_id(1)
    i = pl.program_id(2)
    tq = 2 * tk
    bias2 = bias_ref[h] * LOG2E

    @pl.when(i == 0)
    def _():
        kb_ref[...] = k_ref[...].astype(BF16)
        vb_ref[...] = v_ref[...].astype(BF16)
        kmb_ref[...] = km_ref[...].astype(BF16)
        vmb_ref[...] = vm_ref[...].astype(BF16)

    qb = (q_ref[...] * (hd ** -0.5 * LOG2E)).astype(BF16)
    tri = tri_ref[...]

    def kv(j):
        st = pl.multiple_of(j * tk, tk)
        return kb_ref[pl.ds(st, tk), :], vb_ref[pl.ds(st, tk), :]

    k0, v0 = kv(2 * i)
    k1, v1 = kv(2 * i + 1)
    lower = lax.broadcasted_iota(jnp.int32, (tk, tk), 1) < lax.broadcasted_iota(jnp.int32, (tk, tk), 0)
    cs1, pv1 = _sb_block(qb[tk:], k1, v1, tri, bias2, lower)
    c = jnp.concatenate([jnp.zeros((tk, 1), F32), cs1], axis=0)
    acc = jnp.concatenate([jnp.zeros((tk, hd), F32), pv1], axis=0)
    vis0 = lax.broadcasted_iota(jnp.int32, (tq, tk), 1) < lax.broadcasted_iota(jnp.int32, (tq, tk), 0)
    c, acc = _sb_fold(c, acc, *_sb_block(qb, k0, v0, tri, bias2, vis0))

    def body(t, carry):
        c, acc = carry
        ka, va = kv(2 * i - 1 - 2 * t)
        kb, vb = kv(2 * i - 2 - 2 * t)
        blk_a = _sb_block(qb, ka, va, tri, bias2, None)
        blk_b = _sb_block(qb, kb, vb, tri, bias2, None)
        c, acc = _sb_fold(c, acc, *blk_a)
        return _sb_fold(c, acc, *blk_b)

    c, acc = lax.fori_loop(0, i, body, (c, acc))

    nm = kmb_ref.shape[0]
    mcol = lax.broadcasted_iota(jnp.int32, (tq, nm), 1)
    c, acc = _sb_fold(c, acc, *_sb_block(qb, kmb_ref[...], vmb_ref[...], tri[:nm, :nm], bias2, mcol < n_meta))
    o_ref[...] = acc.astype(o_ref.dtype)


def _prompt_attention(p3, pm_pad, sb_bias, n_heads, hd, n_meta):
    b, s, _ = p3.shape
    aw = n_heads * hd
    tk = ATTN_TILE
    tq = 2 * tk
    assert s % tq == 0
    nm = pm_pad.shape[0]
    qo, ko, vo = aw // hd, 2 * aw // hd, 3 * aw // hd
    tri = (lax.broadcasted_iota(jnp.int32, (tk, tk), 0) > lax.broadcasted_iota(jnp.int32, (tk, tk), 1)).astype(BF16)
    return pl.pallas_call(
        functools.partial(_pattn_kernel, tk=tk, hd=hd, n_meta=n_meta),
        grid=(b, n_heads, s // tq),
        in_specs=[
            pl.BlockSpec(memory_space=pltpu.SMEM),
            pl.BlockSpec((None, tq, hd), lambda bi, h, i: (bi, i, qo + h)),
            pl.BlockSpec((None, s, hd), lambda bi, h, i: (bi, 0, ko + h)),
            pl.BlockSpec((None, s, hd), lambda bi, h, i: (bi, 0, vo + h)),
            pl.BlockSpec((nm, hd), lambda bi, h, i: (0, ko + h)),
            pl.BlockSpec((nm, hd), lambda bi, h, i: (0, vo + h)),
            pl.BlockSpec((tk, tk), lambda bi, h, i: (0, 0)),
        ],
        out_specs=pl.BlockSpec((None, tq, hd), lambda bi, h, i: (bi, i, h)),
        out_shape=jax.ShapeDtypeStruct((b, s, aw), BF16),
        scratch_shapes=[pltpu.VMEM((s, hd), BF16), pltpu.VMEM((s, hd), BF16),
                        pltpu.VMEM((nm, hd), BF16), pltpu.VMEM((nm, hd), BF16)],
        compiler_params=_cparams(("parallel", "arbitrary", "arbitrary")),
        name="prompt_attn",
    )(sb_bias, p3, p3, p3, pm_pad, pm_pad, tri)


def _sattn_kernel(pt_ref, q_ref, bias_past_ref, bias_new_ref, kn_ref, vn_ref, tri_ref, *rest, hd, n_steps):
    page_refs = rest[:2 * PAGES_PER_STEP]
    o_ref, c_ref, acc_ref = rest[2 * PAGES_PER_STEP:]
    s = pl.program_id(1)
    qb = q_ref[...]
    tri = tri_ref[...]
    tb = tri.shape[0]

    def flat(ref):
        return ref[...].reshape(-1, hd).astype(BF16)

    @pl.when(s == 0)
    def _():
        zpad = jnp.zeros((tb - kn_ref.shape[0], hd), BF16)
        kn = jnp.concatenate([kn_ref[...].astype(BF16), zpad], axis=0)
        vn = jnp.concatenate([vn_ref[...].astype(BF16), zpad], axis=0)
        c_ref[...], acc_ref[...] = _sb_chunk(qb, kn, vn, tri, bias_new_ref[...])

    kc = jnp.concatenate([flat(page_refs[r]) for r in range(PAGES_PER_STEP)], axis=0)
    vc = jnp.concatenate([flat(page_refs[PAGES_PER_STEP + r]) for r in range(PAGES_PER_STEP)], axis=0)
    c_ref[...], acc_ref[...] = _sb_fold(c_ref[...], acc_ref[...], *_sb_chunk(qb, kc, vc, tri, bias_past_ref[...]))

    @pl.when(s == n_steps - 1)
    def _():
        o_ref[...] = acc_ref[...]


def _sample_attention(q, k_new, v_new, cache_k, cache_v, page_table, layer, sb_bias):
    db, dt, n_heads, hd = q.shape
    n_pages = page_table.shape[1]
    tb = 2 * PAGE_SIZE
    assert n_pages % PAGES_PER_STEP == 0 and tb % n_heads == 0 and dt * n_heads <= tb
    n_steps = n_pages // PAGES_PER_STEP
    m = n_heads * dt
    qm = (q * (hd ** -0.5 * LOG2E)).transpose(0, 2, 1, 3).reshape(db, m, hd).astype(BF16)
    row_h = jnp.arange(m) // dt
    row_q = jnp.arange(m) % dt
    col_h = jnp.arange(tb) % n_heads
    col_t = jnp.arange(tb) // n_heads
    same_head = row_h[:, None] == col_h[None, :]
    bias_rows = jnp.repeat(sb_bias * LOG2E, dt)[:, None]
    bias_past = jnp.where(same_head, bias_rows, MASKED_LOGIT)
    bias_new = jnp.where(same_head & (col_t[None, :] < row_q[:, None]), bias_rows, MASKED_LOGIT)
    kn = k_new.reshape(db, dt * n_heads, hd)
    vn = v_new.reshape(db, dt * n_heads, hd)
    tri = (lax.broadcasted_iota(jnp.int32, (tb, tb), 0) > lax.broadcasted_iota(jnp.int32, (tb, tb), 1)).astype(BF16)

    def page_spec(r):
        def index(bi, s, pt):
            return (layer, pt[bi * n_pages + (n_pages - (s + 1) * PAGES_PER_STEP + r)], 0, 0, 0)
        return pl.BlockSpec((None, None, PAGE_SIZE, n_heads, hd), index)

    grid_spec = pltpu.PrefetchScalarGridSpec(
        num_scalar_prefetch=1,
        grid=(db, n_steps),
        in_specs=[
            pl.BlockSpec((None, m, hd), lambda bi, s, pt: (bi, 0, 0)),
            pl.BlockSpec((m, tb), lambda bi, s, pt: (0, 0)),
            pl.BlockSpec((m, tb), lambda bi, s, pt: (0, 0)),
            pl.BlockSpec((None, dt * n_heads, hd), lambda bi, s, pt: (bi, 0, 0)),
            pl.BlockSpec((None, dt * n_heads, hd), lambda bi, s, pt: (bi, 0, 0)),
            pl.BlockSpec((tb, tb), lambda bi, s, pt: (0, 0)),
        ] + [page_spec(r) for r in range(PAGES_PER_STEP)] * 2,
        out_specs=pl.BlockSpec((None, m, hd), lambda bi, s, pt: (bi, 0, 0)),
        scratch_shapes=[pltpu.VMEM((m, 1), F32), pltpu.VMEM((m, hd), F32)],
    )
    out = pl.pallas_call(
        functools.partial(_sattn_kernel, hd=hd, n_steps=n_steps),
        grid_spec=grid_spec,
        out_shape=jax.ShapeDtypeStruct((db, m, hd), F32),
        compiler_params=_cparams(("parallel", "arbitrary")),
        name="sample_attn",
    )(page_table.reshape(-1), qm, bias_past, bias_new, kn, vn, tri,
      *([cache_k] * PAGES_PER_STEP), *([cache_v] * PAGES_PER_STEP))
    return out.reshape(db, n_heads, dt, hd).transpose(0, 2, 1, 3)


def _merge_kernel(x_ref, a_ref, o_ref, gp_ref, ga_ref, wbp_ref, wba_ref, wo_ref, y_ref):
    bp = jnp.dot(a_ref[...].astype(BF16), wbp_ref[...], preferred_element_type=F32)
    ba = jnp.dot(o_ref[...].astype(BF16), wba_ref[...], preferred_element_type=F32)
    m = jax.nn.sigmoid(gp_ref[...]) * bp + jax.nn.sigmoid(ga_ref[...]) * ba
    y_ref[...] = x_ref[...] + jnp.dot(m.astype(BF16), wo_ref[...], preferred_element_type=F32)


def _merge(x, a, o, p, w_bp, w_ba, w_out, gate_block):
    t, d = x.shape
    pw, aw = a.shape[1], o.shape[1]
    tm = min(t, 256)
    const = functools.partial(pl.BlockSpec, pipeline_mode=pl.Buffered(1))
    return pl.pallas_call(
        _merge_kernel,
        grid=(pl.cdiv(t, tm),),
        in_specs=[
            pl.BlockSpec((tm, d), lambda i: (i, 0)),
            pl.BlockSpec((tm, pw), lambda i: (i, 0)),
            pl.BlockSpec((tm, aw), lambda i: (i, 0)),
            pl.BlockSpec((tm, d), lambda i: (i, gate_block)),
            pl.BlockSpec((tm, d), lambda i: (i, gate_block + 1)),
            const((pw, d), lambda i: (0, 0)),
            const((aw, d), lambda i: (0, 0)),
            const((d, d), lambda i: (0, 0)),
        ],
        out_specs=pl.BlockSpec((tm, d), lambda i: (i, 0)),
        out_shape=jax.ShapeDtypeStruct((t, d), F32),
        compiler_params=_cparams(("parallel",)),
        name="merge",
    )(x, a, o, p, p, w_bp, w_ba, w_out)


def kernel(x_prompt, x_sample, cache_k, cache_v, state_pool, page_table, meta_tokens, g_ffn1, w_ffn1_gu, w_ffn1_down, g_mix, w_in, g_q, g_k, sb_bias, w_pool, pool_scale, w_branch_pool, w_branch_attn, w_out, g_ffn2, w_ffn2_gu, w_ffn2_down):
    b, s, d = x_prompt.shape
    db, dt, _ = x_sample.shape
    depth, n_heads = sb_bias.shape
    hd = g_q.shape[1]
    aw = n_heads * hd
    pw = state_pool.shape[-1]
    in_w = w_in.shape[-1]
    n_meta = meta_tokens.shape[0]
    past_len = page_table.shape[1] * PAGE_SIZE
    assert depth == 1, "meta rows are dropped after the mixer; a second layer would need them"
    assert pw == aw and in_w == pw + 3 * aw + 2 * d and n_meta == N_META
    l = 0

    ffn1 = _prep_ffn_weights(w_ffn1_gu[l], w_ffn1_down[l])
    ffn2 = _prep_ffn_weights(w_ffn2_gu[l], w_ffn2_down[l])
    w_in_b = w_in[l].astype(BF16)
    w_bp = w_branch_pool[l].astype(BF16)
    w_ba = w_branch_attn[l].astype(BF16)
    w_o = w_out[l].astype(BF16)
    w_pl = w_pool[l].astype(BF16)
    n_chunks = in_w // aw
    head_gain = jnp.ones((n_chunks, 1, aw), F32)
    head_gain = head_gain.at[1, 0].set(jnp.tile(g_q[l], n_heads)).at[2, 0].set(jnp.tile(g_k[l], n_heads))
    gate_block = (pw + 3 * aw) // d

    xr = x_prompt.reshape(b * s, d)
    xe = jnp.concatenate([meta_tokens.astype(F32), x_sample.reshape(db * dt, d)], axis=0)

    x1r = _ffn_half(xr, g_ffn1[l], *ffn1)
    x1e = _ffn_half(xe, g_ffn1[l], *ffn1)
    pr, kr, vr = _in_proj(x1r, g_mix[l], w_in_b, head_gain, n_heads, hd)
    pe, ke, ve = _in_proj(x1e, g_mix[l], w_in_b, head_gain, n_heads, hd)
    pm = pe[:n_meta]
    ps = pe[n_meta:]
    p3 = pr.reshape(b, s, pr.shape[1])
    k4 = kr.reshape(b, s, n_heads, hd)
    v4 = vr.reshape(b, s, n_heads, hd)

    zrow = jnp.zeros((1, pw), F32)
    hist_p = jnp.concatenate([zrow, pm[n_meta - POOL_HIST:, :pw]], axis=0)[None]
    a_p = _pool_mixer(p3, hist_p, w_pl, pool_scale[l], n_meta, pw)
    pm_pad = jnp.pad(pm, ((0, PAGE_SIZE - n_meta), (0, 0)))
    o_p = _prompt_attention(p3, pm_pad, sb_bias[l], n_heads, hd, n_meta)
    x2r = _merge(x1r, a_p.reshape(b * s, pw), o_p.reshape(b * s, aw), pr, w_bp, w_ba, w_o, gate_block)
    y_prompt = _ffn_half(x2r, g_ffn2[l], *ffn2).reshape(b, s, d)

    u_s = ps[:, :pw].reshape(db, dt, pw)
    rows_s = HIST_ROWS * pl.cdiv(dt, HIST_ROWS)
    u_s_pad = jnp.pad(u_s, ((0, 0), (0, rows_s - dt), (0, 0)))
    hist_s = jnp.concatenate([jnp.zeros((db, 1, pw), F32), state_pool[l]], axis=1)
    a_s = _pool_mixer(u_s_pad, hist_s, w_pl, pool_scale[l], past_len, pw)[:, :dt]
    q_s = ps[:, pw:pw + aw].reshape(db, dt, n_heads, hd)
    k_s = ke[n_meta:].reshape(db, dt, n_heads, hd)
    v_s = ve[n_meta:].reshape(db, dt, n_heads, hd)
    o_s = _sample_attention(q_s, k_s, v_s, cache_k, cache_v, page_table, l, sb_bias[l])
    x1s = x1e[n_meta:]
    x2s = _merge(x1s, a_s.reshape(db * dt, pw), o_s.reshape(db * dt, aw).astype(BF16), ps, w_bp, w_ba, w_o, gate_block)
    y_sample = _ffn_half(x2s, g_ffn2[l], *ffn2).reshape(db, dt, d)

    def with_meta(meta, real):
        meta = jnp.broadcast_to(meta[None, :n_meta], (b, n_meta, n_heads, hd))
        return jnp.concatenate([meta, real], axis=1)[None]

    k_prompt = with_meta(ke, k4)
    v_prompt = with_meta(ve, v4)
    pool_prompt = p3[:, s - POOL_HIST:, :pw][None]
    pool_sample = jnp.concatenate([state_pool[l], u_s], axis=1)[:, -POOL_HIST:][None]
    return (y_prompt, y_sample, k_prompt, v_prompt, pool_prompt,
            k_s[None], v_s[None], pool_sample)
```

```python
import functools

import jax
import jax.numpy as jnp
from jax import lax
from jax.experimental import pallas as pl
from jax.experimental.pallas import tpu as pltpu

F32 = jnp.float32
BF16 = jnp.bfloat16

EPS = 1e-6
N_META = 16
POOL_WINDOWS = (2, 4, 8, 16)
POOL_HIST = max(POOL_WINDOWS) - 1
HIST_ROWS = 16
PAGE_SIZE = 128
PAGES_PER_STEP = 8
FF_CHUNK = 512
ROW_TILE = 512
FFN_ROW_TILE = 1024
MXU_COLS = 256
ATTN_TILE = 256
VMEM_LIMIT = 60 * 1024 * 1024


def _cparams(sem):
    return pltpu.CompilerParams(dimension_semantics=sem, vmem_limit_bytes=VMEM_LIMIT)


def _row_tile(t):
    return t if t <= ROW_TILE else ROW_TILE


def _rms_rows(x, g):
    ms = jnp.mean(x * x, axis=-1, keepdims=True)
    return x * lax.rsqrt(ms + EPS) * g


def _ffn_kernel(x_ref, g_ref, wg_ref, wu_ref, wd_ref, o_ref, xn_ref, *, n_col):
    j = pl.program_id(1)

    @pl.when(j == 0)
    def _():
        x = x_ref[...]
        xn_ref[...] = _rms_rows(x, g_ref[...]).astype(BF16)
        o_ref[...] = x

    xn = xn_ref[...]
    a = jnp.dot(xn, wg_ref[...], preferred_element_type=F32)
    b = jnp.dot(xn, wu_ref[...], preferred_element_type=F32)
    act = (((0.5 * a) * jax.nn.sigmoid(a)) * b).astype(BF16)
    cw = o_ref.shape[1] // n_col
    for n in range(n_col):
        cols = slice(n * cw, (n + 1) * cw)
        o_ref[:, cols] += jnp.dot(act, wd_ref[:, cols], preferred_element_type=F32)


def _ffn_half(x, g, wg, wu, wd):
    t, d = x.shape
    tf = FF_CHUNK
    n_ff = wd.shape[0] // tf
    tm = t if t <= FFN_ROW_TILE else FFN_ROW_TILE
    return pl.pallas_call(
        functools.partial(_ffn_kernel, n_col=4),
        grid=(pl.cdiv(t, tm), n_ff),
        in_specs=[
            pl.BlockSpec((tm, d), lambda i, j: (i, 0)),
            pl.BlockSpec((1, d), lambda i, j: (0, 0)),
            pl.BlockSpec((d, tf), lambda i, j: (0, j)),
            pl.BlockSpec((d, tf), lambda i, j: (0, j)),
            pl.BlockSpec((tf, d), lambda i, j: (j, 0)),
        ],
        out_specs=pl.BlockSpec((tm, d), lambda i, j: (i, 0)),
        out_shape=jax.ShapeDtypeStruct((t, d), F32),
        scratch_shapes=[pltpu.VMEM((tm, d), BF16)],
        compiler_params=_cparams(("parallel", "arbitrary")),
        name="ffn_half",
    )(x, g.reshape(1, d), wg, wu, wd)


def _prep_ffn_weights(w_gu, w_down):
    f = w_gu.shape[1] // 2
    pad = FF_CHUNK * pl.cdiv(f, FF_CHUNK) - f
    wg = jnp.pad(w_gu[:, :f].astype(BF16), ((0, 0), (0, pad)))
    wu = jnp.pad(w_gu[:, f:].astype(BF16), ((0, 0), (0, pad)))
    wd = jnp.pad(w_down.astype(BF16), ((0, pad), (0, 0)))
    return wg, wu, wd


def _inproj_kernel(x_ref, g_ref, w_ref, hg_ref, p_ref, k_ref, v_ref, xn_ref, *, n_heads, hd):
    j = pl.program_id(1)

    @pl.when(j == 0)
    def _():
        xn_ref[...] = _rms_rows(x_ref[...], g_ref[...]).astype(BF16)

    heads = [slice(h * hd, (h + 1) * hd) for h in range(n_heads)]

    def projected_heads():
        per_group = max(1, MXU_COLS // hd)
        for h0 in range(0, n_heads, per_group):
            hs = range(h0, min(h0 + per_group, n_heads))
            r = jnp.dot(xn_ref[...], w_ref[:, hs[0] * hd:(hs[-1] + 1) * hd], preferred_element_type=F32)
            for h in hs:
                yield h, heads[h], r[:, (h - h0) * hd:(h - h0 + 1) * hd]

    @pl.when(j == 1)
    def _():
        for h, cols, ph in projected_heads():
            p_ref[:, cols] = _rms_rows(ph, hg_ref[:, cols])

    @pl.when(j == 2)
    def _():
        for h, cols, ph in projected_heads():
            kh = _rms_rows(ph, hg_ref[:, cols])
            p_ref[:, cols] = kh
            k_ref[:, h, :] = kh

    @pl.when(j == 3)
    def _():
        for h, cols, ph in projected_heads():
            p_ref[:, cols] = ph
            v_ref[:, h, :] = ph

    @pl.when(jnp.logical_or(j == 0, j > 3))
    def _():
        p_ref[...] = jnp.dot(xn_ref[...], w_ref[...], preferred_element_type=F32)


def _in_proj(x, g_mix, w_in, head_gain, n_heads, hd):
    t, d = x.shape
    in_w = w_in.shape[1]
    tn = n_heads * hd
    tm = t if t <= FFN_ROW_TILE else FFN_ROW_TILE
    kv_spec = pl.BlockSpec((tm, n_heads, hd), lambda i, j: (i, 0, 0))
    kv_shape = jax.ShapeDtypeStruct((t, n_heads, hd), F32)
    return pl.pallas_call(
        functools.partial(_inproj_kernel, n_heads=n_heads, hd=hd),
        grid=(pl.cdiv(t, tm), in_w // tn),
        in_specs=[
            pl.BlockSpec((tm, d), lambda i, j: (i, 0)),
            pl.BlockSpec((1, d), lambda i, j: (0, 0)),
            pl.BlockSpec((d, tn), lambda i, j: (0, j)),
            pl.BlockSpec((None, 1, tn), lambda i, j: (j, 0, 0)),
        ],
        out_specs=[pl.BlockSpec((tm, tn), lambda i, j: (i, j)), kv_spec, kv_spec],
        out_shape=[jax.ShapeDtypeStruct((t, in_w), F32), kv_shape, kv_shape],
        scratch_shapes=[pltpu.VMEM((tm, d), BF16)],
        compiler_params=_cparams(("parallel", "arbitrary")),
        name="in_proj",
    )(x, g_mix.reshape(1, d), w_in, head_gain)


def _pool_kernel(u_ref, halo_ref, hist_ref, wp_ref, ps_ref, o_ref, *, tm, pos0, group):
    i = pl.program_id(1)
    prev = jnp.where(i == 0, hist_ref[...], halo_ref[...])
    pos = pos0 + i * tm + lax.broadcasted_iota(jnp.int32, (tm, 1), 0)
    for gi, w in enumerate(POOL_WINDOWS):
        cols = slice(gi * group, (gi + 1) * group)
        x = jnp.concatenate([prev[:, cols], u_ref[:, cols]], axis=0)
        s = x
        span = 1
        while span < w:
            s = s + pltpu.roll(s, span, axis=0)
            span *= 2
        count = jnp.minimum(w, pos + 1).astype(F32)
        d = s[HIST_ROWS:] / count - x[HIST_ROWS:]
        y = jnp.dot(d.astype(BF16), wp_ref[gi], preferred_element_type=F32)
        o_ref[:, cols] = (y * ps_ref[:, cols]).astype(o_ref.dtype)


def _pool_mixer(p3, hist, w_pool, pool_scale, pos0, pw):
    b, s, _ = p3.shape
    tm = s if s <= ROW_TILE else ROW_TILE
    n_groups = w_pool.shape[0]
    group = pw // n_groups
    halo_blocks = tm // HIST_ROWS
    per_batch = hist.shape[0] != 1
    return pl.pallas_call(
        functools.partial(_pool_kernel, tm=tm, pos0=pos0, group=group),
        grid=(b, s // tm),
        in_specs=[
            pl.BlockSpec((None, tm, pw), lambda bi, i: (bi, i, 0)),
            pl.BlockSpec((None, HIST_ROWS, pw), lambda bi, i: (bi, jnp.maximum(i * halo_blocks - 1, 0), 0)),
            pl.BlockSpec((None, HIST_ROWS, pw), (lambda bi, i: (bi, 0, 0)) if per_batch else (lambda bi, i: (0, 0, 0))),
            pl.BlockSpec((n_groups, group, group), lambda bi, i: (0, 0, 0)),
            pl.BlockSpec((1, pw), lambda bi, i: (0, 0)),
        ],
        out_specs=pl.BlockSpec((None, tm, pw), lambda bi, i: (bi, i, 0)),
        out_shape=jax.ShapeDtypeStruct((b, s, pw), BF16),
        compiler_params=_cparams(("parallel", "arbitrary")),
        name="pool_mixer",
    )(p3, p3, hist, w_pool, pool_scale.reshape(1, pw))


LOG2E = 1.4426950408889634
MASKED_LOGIT = -1e30


EXP2_CLAMP = 126.0


def _neg_log2_stay(z):
    return jnp.maximum(jnp.log2(1.0 + jnp.exp2(jnp.minimum(z, EXP2_CLAMP))), z)


def _sb_block(qb, kb, vb, tri, bias2, mask):
    z = lax.dot_general(qb, kb, (((1,), (1,)), ((), ())), preferred_element_type=F32) + bias2
    nls = _neg_log2_stay(z)
    nls_vis = nls if mask is None else jnp.where(mask, nls, 0.0)
    later = jnp.dot(nls_vis.astype(BF16), tri, preferred_element_type=F32)
    w = jnp.exp2(z - nls - later)
    if mask is not None:
        w = jnp.where(mask, w, 0.0)
    pv = jnp.dot(w.astype(BF16), vb, preferred_element_type=F32)
    csum = later[:, :1] + nls_vis[:, :1]
    return csum, pv


def _sb_chunk(qb, kc, vc, tri, bias_tile):
    m = qb.shape[0]
    tb = tri.shape[0]
    n_sub = kc.shape[0] // tb
    z = lax.dot_general(qb, kc, (((1,), (1,)), ((), ())), preferred_element_type=F32)
    z = z + jnp.concatenate([bias_tile] * n_sub, axis=1)
    nls = _neg_log2_stay(z)
    nls_rows = jnp.concatenate([nls[:, j * tb:(j + 1) * tb] for j in range(n_sub)], axis=0)
    later_rows = jnp.dot(nls_rows.astype(BF16), tri, preferred_element_type=F32)
    sub_sum = later_rows[:, :1] + nls_rows[:, :1]
    parts = []
    tail = jnp.zeros((m, 1), F32)
    for j in reversed(range(n_sub)):
        parts.append(later_rows[j * m:(j + 1) * m] + tail)
        tail = tail + sub_sum[j * m:(j + 1) * m]
    later = jnp.concatenate(parts[::-1], axis=1)
    w = jnp.exp2(z - nls - later)
    pv = jnp.dot(w.astype(BF16), vc, preferred_element_type=F32)
    return tail, pv


def _sb_fold(c, acc, csum, pv):
    return c + csum, acc + jnp.exp2(-c) * pv


def _prompt_tile(h, i, bias_ref, q_ref, tri_ref, o_ref, kb_ref, vb_ref, kmb_ref, vmb_ref, tk, hd):
    tq = 2 * tk
    bias2 = bias_ref[h] * LOG2E
    qb = (q_ref[...] * (hd ** -0.5 * LOG2E)).astype(BF16)
    tri = tri_ref[...]

    def kv(j):
        st = pl.multiple_of(j * tk, tk)
        return kb_ref[pl.ds(st, tk), :], vb_ref[pl.ds(st, tk), :]

    nm = kmb_ref.shape[0]
    zt = lax.dot_general(kmb_ref[...], qb, (((1,), (1,)), ((), ())), preferred_element_type=F32) + bias2
    nlst = _neg_log2_stay(zt)
    later_keys = (lax.broadcasted_iota(jnp.int32, (nm, nm), 0) < lax.broadcasted_iota(jnp.int32, (nm, nm), 1))
    latert = jnp.dot(jnp.where(later_keys, 1.0, 0.0).astype(BF16), nlst.astype(BF16), preferred_element_type=F32)
    wt = jnp.exp2(zt - nlst - latert).astype(BF16)
    pv_meta = lax.dot_general(wt, vmb_ref[...], (((0,), (0,)), ((), ())), preferred_element_type=F32)

    k0, v0 = kv(2 * i)
    k1, v1 = kv(2 * i + 1)
    lower = lax.broadcasted_iota(jnp.int32, (tk, tk), 1) < lax.broadcasted_iota(jnp.int32, (tk, tk), 0)
    cs1, pv1 = _sb_block(qb[tk:], k1, v1, tri, bias2, lower)
    c = jnp.concatenate([jnp.zeros((tk, 1), F32), cs1], axis=0)
    acc = jnp.concatenate([jnp.zeros((tk, hd), F32), pv1], axis=0)
    vis0 = lax.broadcasted_iota(jnp.int32, (tq, tk), 1) < lax.broadcasted_iota(jnp.int32, (tq, tk), 0)
    c, acc = _sb_fold(c, acc, *_sb_block(qb, k0, v0, tri, bias2, vis0))

    def body(t, carry):
        c, acc = carry
        ka, va = kv(2 * i - 1 - 2 * t)
        kb, vb = kv(2 * i - 2 - 2 * t)
        blk_a = _sb_block(qb, ka, va, tri, bias2, None)
        blk_b = _sb_block(qb, kb, vb, tri, bias2, None)
        c, acc = _sb_fold(c, acc, *blk_a)
        return _sb_fold(c, acc, *blk_b)

    c, acc = lax.fori_loop(0, i, body, (c, acc))
    o_ref[...] = (acc + jnp.exp2(-c) * pv_meta).astype(o_ref.dtype)


def _attention(p3, pm, q, k_new, v_new, cache_k, cache_v, page_table, layer, sb_bias):
    b, s, _ = p3.shape
    db, dt, n_heads, hd = q.shape
    aw = n_heads * hd
    tk = ATTN_TILE
    tq = 2 * tk
    nm = pm.shape[0]
    n_pages = page_table.shape[1]
    tb = tk
    assert s % tq == 0 and tb == 2 * PAGE_SIZE
    assert n_pages % PAGES_PER_STEP == 0 and tb % n_heads == 0 and dt * n_heads <= tb
    n_steps = n_pages // PAGES_PER_STEP
    grid = (b, n_heads, s // tq)
    total = db * n_steps
    assert total <= grid[0] * grid[1] * grid[2], "more sample steps than prompt tiles to pair them with"
    qo, ko, vo = aw // hd, 2 * aw // hd, 3 * aw // hd
    tri = (lax.broadcasted_iota(jnp.int32, (tk, tk), 0) > lax.broadcasted_iota(jnp.int32, (tk, tk), 1)).astype(BF16)

    m = n_heads * dt
    qm = (q * (hd ** -0.5 * LOG2E)).transpose(0, 2, 1, 3).reshape(db, m, hd).astype(BF16)
    row_h = jnp.arange(m) // dt
    row_q = jnp.arange(m) % dt
    col_h = jnp.arange(tb) % n_heads
    col_t = jnp.arange(tb) // n_heads
    same_head = row_h[:, None] == col_h[None, :]
    bias_rows = jnp.repeat(sb_bias * LOG2E, dt)[:, None]
    bias_past = jnp.where(same_head, bias_rows, MASKED_LOGIT)
    bias_new = jnp.where(same_head & (col_t[None, :] < row_q[:, None]), bias_rows, MASKED_LOGIT)
    kn = k_new.reshape(db, dt * n_heads, hd)
    vn = v_new.reshape(db, dt * n_heads, hd)

    def sample_step(bi, h, i):
        n = jnp.minimum((bi * grid[1] + h) * grid[2] + i, total - 1)
        return n // n_steps, lax.rem(n, n_steps)

    def seq_block(bi, h, i, pt):
        return (sample_step(bi, h, i)[0], 0, 0)

    def page_spec(r):
        def index(bi, h, i, pt):
            seq, st = sample_step(bi, h, i)
            return (layer, pt[seq * n_pages + (n_pages - (st + 1) * PAGES_PER_STEP + r)], 0, 0, 0)
        return pl.BlockSpec((None, None, PAGE_SIZE, n_heads, hd), index)

    grid_spec = pltpu.PrefetchScalarGridSpec(
        num_scalar_prefetch=1,
        grid=grid,
        in_specs=[
            pl.BlockSpec(memory_space=pltpu.SMEM),
            pl.BlockSpec((None, tq, hd), lambda bi, h, i, pt: (bi, i, qo + h)),
            pl.BlockSpec((None, s, hd), lambda bi, h, i, pt: (bi, 0, ko + h)),
            pl.BlockSpec((None, s, hd), lambda bi, h, i, pt: (bi, 0, vo + h)),
            pl.BlockSpec((nm, hd), lambda bi, h, i, pt: (0, ko + h)),
            pl.BlockSpec((nm, hd), lambda bi, h, i, pt: (0, vo + h)),
            pl.BlockSpec((tk, tk), lambda bi, h, i, pt: (0, 0)),
            pl.BlockSpec((None, m, hd), seq_block),
            pl.BlockSpec((m, tb), lambda bi, h, i, pt: (0, 0)),
            pl.BlockSpec((m, tb), lambda bi, h, i, pt: (0, 0)),
            pl.BlockSpec((None, dt * n_heads, hd), seq_block),
            pl.BlockSpec((None, dt * n_heads, hd), seq_block),
        ] + [page_spec(r) for r in range(PAGES_PER_STEP)] * 2,
        out_specs=[pl.BlockSpec((None, tq, hd), lambda bi, h, i, pt: (bi, i, h)),
                   pl.BlockSpec((None, m, hd), seq_block)],
        scratch_shapes=[pltpu.VMEM((s, hd), BF16), pltpu.VMEM((s, hd), BF16),
                        pltpu.VMEM((nm, hd), BF16), pltpu.VMEM((nm, hd), BF16),
                        pltpu.VMEM((m, 1), F32), pltpu.VMEM((m, hd), F32)],
    )
    o_p, o_s = pl.pallas_call(
        functools.partial(_attn_kernel, tk=tk, hd=hd, grid=grid, n_steps=n_steps, n_seq=db),
        grid_spec=grid_spec,
        out_shape=[jax.ShapeDtypeStruct((b, s, aw), BF16), jax.ShapeDtypeStruct((db, m, hd), F32)],
        compiler_params=_cparams(("arbitrary", "arbitrary", "arbitrary")),
        name="attention",
    )(page_table.reshape(-1), sb_bias, p3, p3, p3, pm, pm, tri, qm, bias_past, bias_new, kn, vn,
      *([cache_k] * PAGES_PER_STEP), *([cache_v] * PAGES_PER_STEP))
    return o_p, o_s.reshape(db, n_heads, dt, hd).transpose(0, 2, 1, 3)


def _sample_start(q_ref, bias_new_ref, kn_ref, vn_ref, tri_ref, c_ref, acc_ref):
    tri = tri_ref[...]
    zpad = jnp.zeros((tri.shape[0] - kn_ref.shape[0], kn_ref.shape[1]), BF16)
    kn = jnp.concatenate([kn_ref[...].astype(BF16), zpad], axis=0)
    vn = jnp.concatenate([vn_ref[...].astype(BF16), zpad], axis=0)
    c_ref[...], acc_ref[...] = _sb_chunk(q_ref[...], kn, vn, tri, bias_new_ref[...])


def _sample_step(active, q_ref, bias_past_ref, tri_ref, page_refs, c_ref, acc_ref):
    def flat(ref):
        return ref[...].reshape(-1, ref.shape[-1]).astype(BF16)

    kc = jnp.concatenate([flat(page_refs[r]) for r in range(PAGES_PER_STEP)], axis=0)
    vc = jnp.concatenate([flat(page_refs[PAGES_PER_STEP + r]) for r in range(PAGES_PER_STEP)], axis=0)
    c, acc = _sb_fold(c_ref[...], acc_ref[...], *_sb_chunk(q_ref[...], kc, vc, tri_ref[...], bias_past_ref[...]))
    if active is not None:
        c = jnp.where(active, c, c_ref[...])
        acc = jnp.where(active, acc, acc_ref[...])
    c_ref[...] = c
    acc_ref[...] = acc


def _attn_kernel(pt_ref, bias_ref, q_ref, k_ref, v_ref, km_ref, vm_ref, tri_ref,
                 sq_ref, bias_past_ref, bias_new_ref, kn_ref, vn_ref, *rest, tk, hd, grid, n_steps, n_seq):
    page_refs = rest[:2 * PAGES_PER_STEP]
    o_ref, so_ref, kb_ref, vb_ref, kmb_ref, vmb_ref, c_ref, acc_ref = rest[2 * PAGES_PER_STEP:]
    h = pl.program_id(1)
    i = pl.program_id(2)
    n = (pl.program_id(0) * grid[1] + h) * grid[2] + i
    total = n_seq * n_steps
    active = None if total == grid[0] * grid[1] * grid[2] else n < total
    st = lax.rem(jnp.minimum(n, total - 1), n_steps)

    @pl.when(i == 0)
    def _():
        kb_ref[...] = k_ref[...].astype(BF16)
        vb_ref[...] = v_ref[...].astype(BF16)
        kmb_ref[...] = km_ref[...].astype(BF16)
        vmb_ref[...] = vm_ref[...].astype(BF16)

    @pl.when(st == 0 if active is None else jnp.logical_and(st == 0, active))
    def _():
        _sample_start(sq_ref, bias_new_ref, kn_ref, vn_ref, tri_ref, c_ref, acc_ref)

    _sample_step(active, sq_ref, bias_past_ref, tri_ref, page_refs, c_ref, acc_ref)
    _prompt_tile(h, i, bias_ref, q_ref, tri_ref, o_ref, kb_ref, vb_ref, kmb_ref, vmb_ref, tk, hd)

    @pl.when(st == n_steps - 1)
    def _():
        so_ref[...] = acc_ref[...]


def _merge_kernel(x_ref, a_ref, o_ref, gp_ref, ga_ref, wbp_ref, wba_ref, wo_ref, y_ref):
    bp = jnp.dot(a_ref[...].astype(BF16), wbp_ref[...], preferred_element_type=F32)
    ba = jnp.dot(o_ref[...].astype(BF16), wba_ref[...], preferred_element_type=F32)
    m = jax.nn.sigmoid(gp_ref[...]) * bp + jax.nn.sigmoid(ga_ref[...]) * ba
    y_ref[...] = x_ref[...] + jnp.dot(m.astype(BF16), wo_ref[...], preferred_element_type=F32)


def _merge(x, a, o, p, w_bp, w_ba, w_out, gate_block):
    t, d = x.shape
    pw, aw = a.shape[1], o.shape[1]
    tm = min(t, 256)
    const = functools.partial(pl.BlockSpec, pipeline_mode=pl.Buffered(1))
    return pl.pallas_call(
        _merge_kernel,
        grid=(pl.cdiv(t, tm),),
        in_specs=[
            pl.BlockSpec((tm, d), lambda i: (i, 0)),
            pl.BlockSpec((tm, pw), lambda i: (i, 0)),
            pl.BlockSpec((tm, aw), lambda i: (i, 0)),
            pl.BlockSpec((tm, d), lambda i: (i, gate_block)),
            pl.BlockSpec((tm, d), lambda i: (i, gate_block + 1)),
            const((pw, d), lambda i: (0, 0)),
            const((aw, d), lambda i: (0, 0)),
            const((d, d), lambda i: (0, 0)),
        ],
        out_specs=pl.BlockSpec((tm, d), lambda i: (i, 0)),
        out_shape=jax.ShapeDtypeStruct((t, d), F32),
        compiler_params=_cparams(("parallel",)),
        name="merge",
    )(x, a, o, p, p, w_bp, w_ba, w_out)


def kernel(x_prompt, x_sample, cache_k, cache_v, state_pool, page_table, meta_tokens, g_ffn1, w_ffn1_gu, w_ffn1_down, g_mix, w_in, g_q, g_k, sb_bias, w_pool, pool_scale, w_branch_pool, w_branch_attn, w_out, g_ffn2, w_ffn2_gu, w_ffn2_down):
    b, s, d = x_prompt.shape
    db, dt, _ = x_sample.shape
    depth, n_heads = sb_bias.shape
    hd = g_q.shape[1]
    aw = n_heads * hd
    pw = state_pool.shape[-1]
    in_w = w_in.shape[-1]
    n_meta = meta_tokens.shape[0]
    past_len = page_table.shape[1] * PAGE_SIZE
    assert depth == 1, "meta rows are dropped after the mixer; a second layer would need them"
    assert pw == aw and in_w == pw + 3 * aw + 2 * d and n_meta == N_META
    l = 0

    ffn1 = _prep_ffn_weights(w_ffn1_gu[l], w_ffn1_down[l])
    ffn2 = _prep_ffn_weights(w_ffn2_gu[l], w_ffn2_down[l])
    w_in_b = w_in[l].astype(BF16)
    w_bp = w_branch_pool[l].astype(BF16)
    w_ba = w_branch_attn[l].astype(BF16)
    w_o = w_out[l].astype(BF16)
    w_pl = w_pool[l].astype(BF16)
    n_chunks = in_w // aw
    head_gain = jnp.ones((n_chunks, 1, aw), F32)
    head_gain = head_gain.at[1, 0].set(jnp.tile(g_q[l], n_heads)).at[2, 0].set(jnp.tile(g_k[l], n_heads))
    gate_block = (pw + 3 * aw) // d

    xr = x_prompt.reshape(b * s, d)
    xe = jnp.concatenate([meta_tokens.astype(F32), x_sample.reshape(db * dt, d)], axis=0)

    x1r = _ffn_half(xr, g_ffn1[l], *ffn1)
    x1e = _ffn_half(xe, g_ffn1[l], *ffn1)
    pr, kr, vr = _in_proj(x1r, g_mix[l], w_in_b, head_gain, n_heads, hd)
    pe, ke, ve = _in_proj(x1e, g_mix[l], w_in_b, head_gain, n_heads, hd)
    pm = pe[:n_meta]
    ps = pe[n_meta:]
    p3 = pr.reshape(b, s, pr.shape[1])
    k4 = kr.reshape(b, s, n_heads, hd)
    v4 = vr.reshape(b, s, n_heads, hd)

    q_s = ps[:, pw:pw + aw].reshape(db, dt, n_heads, hd)
    k_s = ke[n_meta:].reshape(db, dt, n_heads, hd)
    v_s = ve[n_meta:].reshape(db, dt, n_heads, hd)
    o_p, o_s = _attention(p3, pm, q_s, k_s, v_s, cache_k, cache_v, page_table, l, sb_bias[l])

    zrow = jnp.zeros((1, pw), F32)
    hist_p = jnp.concatenate([zrow, pm[n_meta - POOL_HIST:, :pw]], axis=0)[None]
    a_p = _pool_mixer(p3, hist_p, w_pl, pool_scale[l], n_meta, pw)
    x2r = _merge(x1r, a_p.reshape(b * s, pw), o_p.reshape(b * s, aw), pr, w_bp, w_ba, w_o, gate_block)
    y_prompt = _ffn_half(x2r, g_ffn2[l], *ffn2).reshape(b, s, d)

    u_s = ps[:, :pw].reshape(db, dt, pw)
    rows_s = HIST_ROWS * pl.cdiv(dt, HIST_ROWS)
    u_s_pad = jnp.pad(u_s, ((0, 0), (0, rows_s - dt), (0, 0)))
    hist_s = jnp.concatenate([jnp.zeros((db, 1, pw), F32), state_pool[l]], axis=1)
    a_s = _pool_mixer(u_s_pad, hist_s, w_pl, pool_scale[l], past_len, pw)[:, :dt]
    x1s = x1e[n_meta:]
    x2s = _merge(x1s, a_s.reshape(db * dt, pw), o_s.reshape(db * dt, aw).astype(BF16), ps, w_bp, w_ba, w_o, gate_block)
    y_sample = _ffn_half(x2s, g_ffn2[l], *ffn2).reshape(db, dt, d)

    def with_meta(meta, real):
        meta = jnp.broadcast_to(meta[None, :n_meta], (b, n_meta, n_heads, hd))
        return jnp.concatenate([meta, real], axis=1)[None]

    k_prompt = with_meta(ke, k4)
    v_prompt = with_meta(ve, v4)
    pool_prompt = p3[:, s - POOL_HIST:, :pw][None]
    pool_sample = jnp.concatenate([state_pool[l], u_s], axis=1)[:, -POOL_HIST:][None]
    return (y_prompt, y_sample, k_prompt, v_prompt, pool_prompt,
            k_s[None], v_s[None], pool_sample)
```

```python
import functools

import jax
import jax.numpy as jnp
from jax import lax
from jax.experimental import pallas as pl
from jax.experimental.pallas import tpu as pltpu

F32 = jnp.float32
BF16 = jnp.bfloat16

EPS = 1e-6
N_META = 16
POOL_WINDOWS = (2, 4, 8, 16)
POOL_HIST = max(POOL_WINDOWS) - 1
HIST_ROWS = 16
PAGE_SIZE = 128
PAGES_PER_STEP = 16
FF_CHUNK = 512
ROW_TILE = 512
FFN_ROW_TILE = 1024
MXU_COLS = 256
ATTN_TILE = 256
ATTN_Q_BLOCKS = 4
VMEM_LIMIT = 60 * 1024 * 1024


def _cparams(sem):
    return pltpu.CompilerParams(dimension_semantics=sem, vmem_limit_bytes=VMEM_LIMIT)


def _row_tile(t):
    return t if t <= ROW_TILE else ROW_TILE


def _rms_rows(x, g):
    ms = jnp.mean(x * x, axis=-1, keepdims=True)
    return x * lax.rsqrt(ms + EPS) * g


def _ffn_kernel(x_ref, g_ref, wg_ref, wu_ref, wd_ref, o_ref, xn_ref, *, n_col):
    j = pl.program_id(1)

    @pl.when(j == 0)
    def _():
        x = x_ref[...]
        xn_ref[...] = _rms_rows(x, g_ref[...]).astype(BF16)
        o_ref[...] = x

    xn = xn_ref[...]
    a = jnp.dot(xn, wg_ref[...], preferred_element_type=F32)
    b = jnp.dot(xn, wu_ref[...], preferred_element_type=F32)
    act = (((0.5 * a) * jax.nn.sigmoid(a)) * b).astype(BF16)
    cw = o_ref.shape[1] // n_col
    for n in range(n_col):
        cols = slice(n * cw, (n + 1) * cw)
        o_ref[:, cols] += jnp.dot(act, wd_ref[:, cols], preferred_element_type=F32)


def _ffn_half(x, g, wg, wu, wd):
    t, d = x.shape
    tf = FF_CHUNK
    n_ff = wd.shape[0] // tf
    tm = t if t <= FFN_ROW_TILE else FFN_ROW_TILE
    return pl.pallas_call(
        functools.partial(_ffn_kernel, n_col=4),
        grid=(pl.cdiv(t, tm), n_ff),
        in_specs=[
            pl.BlockSpec((tm, d), lambda i, j: (i, 0)),
            pl.BlockSpec((1, d), lambda i, j: (0, 0)),
            pl.BlockSpec((d, tf), lambda i, j: (0, j)),
            pl.BlockSpec((d, tf), lambda i, j: (0, j)),
            pl.BlockSpec((tf, d), lambda i, j: (j, 0)),
        ],
        out_specs=pl.BlockSpec((tm, d), lambda i, j: (i, 0)),
        out_shape=jax.ShapeDtypeStruct((t, d), F32),
        scratch_shapes=[pltpu.VMEM((tm, d), BF16)],
        compiler_params=_cparams(("parallel", "arbitrary")),
        name="ffn_half",
    )(x, g.reshape(1, d), wg, wu, wd)


def _prep_ffn_weights(w_gu, w_down):
    f = w_gu.shape[1] // 2
    pad = FF_CHUNK * pl.cdiv(f, FF_CHUNK) - f
    wg = jnp.pad(w_gu[:, :f].astype(BF16), ((0, 0), (0, pad)))
    wu = jnp.pad(w_gu[:, f:].astype(BF16), ((0, 0), (0, pad)))
    wd = jnp.pad(w_down.astype(BF16), ((0, pad), (0, 0)))
    return wg, wu, wd


def _inproj_kernel(x_ref, g_ref, w_ref, hg_ref, p_ref, k_ref, v_ref, xn_ref, *, n_heads, hd):
    j = pl.program_id(1)

    @pl.when(j == 0)
    def _():
        xn_ref[...] = _rms_rows(x_ref[...], g_ref[...]).astype(BF16)

    heads = [slice(h * hd, (h + 1) * hd) for h in range(n_heads)]

    def projected_heads():
        per_group = max(1, MXU_COLS // hd)
        for h0 in range(0, n_heads, per_group):
            hs = range(h0, min(h0 + per_group, n_heads))
            r = jnp.dot(xn_ref[...], w_ref[:, hs[0] * hd:(hs[-1] + 1) * hd], preferred_element_type=F32)
            for h in hs:
                yield h, heads[h], r[:, (h - h0) * hd:(h - h0 + 1) * hd]

    @pl.when(j == 1)
    def _():
        for h, cols, ph in projected_heads():
            p_ref[:, cols] = _rms_rows(ph, hg_ref[:, cols])

    @pl.when(j == 2)
    def _():
        for h, cols, ph in projected_heads():
            kh = _rms_rows(ph, hg_ref[:, cols])
            p_ref[:, cols] = kh
            k_ref[:, h, :] = kh

    @pl.when(j == 3)
    def _():
        for h, cols, ph in projected_heads():
            p_ref[:, cols] = ph
            v_ref[:, h, :] = ph

    @pl.when(jnp.logical_or(j == 0, j > 3))
    def _():
        p_ref[...] = jnp.dot(xn_ref[...], w_ref[...], preferred_element_type=F32)


def _in_proj(x, g_mix, w_in, head_gain, n_heads, hd):
    t, d = x.shape
    in_w = w_in.shape[1]
    tn = n_heads * hd
    tm = t if t <= FFN_ROW_TILE else FFN_ROW_TILE
    kv_spec = pl.BlockSpec((tm, n_heads, hd), lambda i, j: (i, 0, 0))
    kv_shape = jax.ShapeDtypeStruct((t, n_heads, hd), F32)
    return pl.pallas_call(
        functools.partial(_inproj_kernel, n_heads=n_heads, hd=hd),
        grid=(pl.cdiv(t, tm), in_w // tn),
        in_specs=[
            pl.BlockSpec((tm, d), lambda i, j: (i, 0)),
            pl.BlockSpec((1, d), lambda i, j: (0, 0)),
            pl.BlockSpec((d, tn), lambda i, j: (0, j)),
            pl.BlockSpec((None, 1, tn), lambda i, j: (j, 0, 0)),
        ],
        out_specs=[pl.BlockSpec((tm, tn), lambda i, j: (i, j)), kv_spec, kv_spec],
        out_shape=[jax.ShapeDtypeStruct((t, in_w), F32), kv_shape, kv_shape],
        scratch_shapes=[pltpu.VMEM((tm, d), BF16)],
        compiler_params=_cparams(("parallel", "arbitrary")),
        name="in_proj",
    )(x, g_mix.reshape(1, d), w_in, head_gain)


def _pool_kernel(u_ref, halo_ref, hist_ref, wp_ref, ps_ref, o_ref, *, tm, pos0, group):
    i = pl.program_id(1)
    prev = jnp.where(i == 0, hist_ref[...], halo_ref[...])
    pos = pos0 + i * tm + lax.broadcasted_iota(jnp.int32, (tm, 1), 0)
    for gi, w in enumerate(POOL_WINDOWS):
        cols = slice(gi * group, (gi + 1) * group)
        x = jnp.concatenate([prev[:, cols], u_ref[:, cols]], axis=0)
        s = x
        span = 1
        while span < w:
            s = s + pltpu.roll(s, span, axis=0)
            span *= 2
        count = jnp.minimum(w, pos + 1).astype(F32)
        d = s[HIST_ROWS:] / count - x[HIST_ROWS:]
        y = jnp.dot(d.astype(BF16), wp_ref[gi], preferred_element_type=F32)
        o_ref[:, cols] = (y * ps_ref[:, cols]).astype(o_ref.dtype)


def _pool_mixer(p3, hist, w_pool, pool_scale, pos0, pw):
    b, s, _ = p3.shape
    tm = s if s <= ROW_TILE else ROW_TILE
    n_groups = w_pool.shape[0]
    group = pw // n_groups
    halo_blocks = tm // HIST_ROWS
    per_batch = hist.shape[0] != 1
    return pl.pallas_call(
        functools.partial(_pool_kernel, tm=tm, pos0=pos0, group=group),
        grid=(b, s // tm),
        in_specs=[
            pl.BlockSpec((None, tm, pw), lambda bi, i: (bi, i, 0)),
            pl.BlockSpec((None, HIST_ROWS, pw), lambda bi, i: (bi, jnp.maximum(i * halo_blocks - 1, 0), 0)),
            pl.BlockSpec((None, HIST_ROWS, pw), (lambda bi, i: (bi, 0, 0)) if per_batch else (lambda bi, i: (0, 0, 0))),
            pl.BlockSpec((n_groups, group, group), lambda bi, i: (0, 0, 0)),
            pl.BlockSpec((1, pw), lambda bi, i: (0, 0)),
        ],
        out_specs=pl.BlockSpec((None, tm, pw), lambda bi, i: (bi, i, 0)),
        out_shape=jax.ShapeDtypeStruct((b, s, pw), BF16),
        compiler_params=_cparams(("parallel", "arbitrary")),
        name="pool_mixer",
    )(p3, p3, hist, w_pool, pool_scale.reshape(1, pw))


LOG2E = 1.4426950408889634
MASKED_LOGIT = -1e30


EXP2_CLAMP = 126.0


def _neg_log2_stay(z):
    return jnp.maximum(jnp.log2(1.0 + jnp.exp2(jnp.minimum(z, EXP2_CLAMP))), z)


def _sb_block(qb, kb, vb, tri, bias2, mask):
    z = lax.dot_general(qb, kb, (((1,), (1,)), ((), ())), preferred_element_type=F32) + bias2
    nls = _neg_log2_stay(z)
    nls_vis = nls if mask is None else jnp.where(mask, nls, 0.0)
    later = jnp.dot(nls_vis.astype(BF16), tri, preferred_element_type=F32)
    w = jnp.exp2(z - nls - later)
    if mask is not None:
        w = jnp.where(mask, w, 0.0)
    pv = jnp.dot(w.astype(BF16), vb, preferred_element_type=F32)
    csum = later[:, :1] + nls_vis[:, :1]
    return csum, pv


def _sb_chunk(qb, kc, vc, tri, bias_tile):
    m = qb.shape[0]
    tb = tri.shape[0]
    n_sub = kc.shape[0] // tb
    z = lax.dot_general(qb, kc, (((1,), (1,)), ((), ())), preferred_element_type=F32)
    z = z + jnp.concatenate([bias_tile] * n_sub, axis=1)
    nls = _neg_log2_stay(z)
    nls_rows = jnp.concatenate([nls[:, j * tb:(j + 1) * tb] for j in range(n_sub)], axis=0)
    later_rows = jnp.dot(nls_rows.astype(BF16), tri, preferred_element_type=F32)
    sub_sum = later_rows[:, :1] + nls_rows[:, :1]
    parts = []
    tail = jnp.zeros((m, 1), F32)
    for j in reversed(range(n_sub)):
        parts.append(later_rows[j * m:(j + 1) * m] + tail)
        tail = tail + sub_sum[j * m:(j + 1) * m]
    later = jnp.concatenate(parts[::-1], axis=1)
    w = jnp.exp2(z - nls - later)
    pv = jnp.dot(w.astype(BF16), vc, preferred_element_type=F32)
    return tail, pv


def _sb_fold(c, acc, csum, pv):
    return c + csum, acc + jnp.exp2(-c) * pv


def _prompt_tile(h, i, bias_ref, q_ref, tri_ref, o_ref, kb_ref, vb_ref, kmb_ref, vmb_ref, tk, hd):
    tq = q_ref.shape[0]
    nb = tq // tk
    bias2 = bias_ref[h] * LOG2E
    qb = (q_ref[...] * (hd ** -0.5 * LOG2E)).astype(BF16)
    tri = tri_ref[...]

    def kv(j):
        st = pl.multiple_of(j * tk, tk)
        return kb_ref[pl.ds(st, tk), :], vb_ref[pl.ds(st, tk), :]

    nm = kmb_ref.shape[0]
    zt = lax.dot_general(kmb_ref[...], qb, (((1,), (1,)), ((), ())), preferred_element_type=F32) + bias2
    nlst = _neg_log2_stay(zt)
    later_keys = (lax.broadcasted_iota(jnp.int32, (nm, nm), 0) < lax.broadcasted_iota(jnp.int32, (nm, nm), 1))
    latert = jnp.dot(jnp.where(later_keys, 1.0, 0.0).astype(BF16), nlst.astype(BF16), preferred_element_type=F32)
    wt = jnp.exp2(zt - nlst - latert).astype(BF16)
    pv_meta = lax.dot_general(wt, vmb_ref[...], (((0,), (0,)), ((), ())), preferred_element_type=F32)

    lower = lax.broadcasted_iota(jnp.int32, (tk, tk), 1) < lax.broadcasted_iota(jnp.int32, (tk, tk), 0)
    state = [None] * nb
    for jj in reversed(range(nb)):
        kj, vj = kv(nb * i + jj)
        state[jj] = _sb_block(qb[jj * tk:(jj + 1) * tk], kj, vj, tri, bias2, lower)
        if jj + 1 < nb:
            cs, pv = _sb_block(qb[(jj + 1) * tk:], kj, vj, tri, bias2, None)
            for r in range(jj + 1, nb):
                rows = slice((r - jj - 1) * tk, (r - jj) * tk)
                state[r] = _sb_fold(*state[r], cs[rows], pv[rows])
    c = jnp.concatenate([st[0] for st in state], axis=0)
    acc = jnp.concatenate([st[1] for st in state], axis=0)

    def body(t, carry):
        blocks = [_sb_block(qb, *kv(nb * (i - t) - 1 - u), tri, bias2, None) for u in range(nb)]
        for blk in blocks:
            carry = _sb_fold(*carry, *blk)
        return carry

    c, acc = lax.fori_loop(0, i, body, (c, acc))
    o_ref[...] = (acc + jnp.exp2(-c) * pv_meta).astype(o_ref.dtype)


def _attention(p3, pm, q, k_new, v_new, cache_k, cache_v, page_table, layer, sb_bias):
    b, s, _ = p3.shape
    db, dt, n_heads, hd = q.shape
    aw = n_heads * hd
    tk = ATTN_TILE
    tq = ATTN_Q_BLOCKS * tk
    nm = pm.shape[0]
    n_pages = page_table.shape[1]
    tb = tk
    assert s % tq == 0 and tb == 2 * PAGE_SIZE
    assert n_pages % PAGES_PER_STEP == 0 and tb % n_heads == 0 and dt * n_heads <= tb
    n_steps = n_pages // PAGES_PER_STEP
    grid = (b, n_heads, s // tq)
    total = db * n_steps
    assert total <= grid[0] * grid[1] * grid[2], "more sample steps than prompt tiles to pair them with"
    qo, ko, vo = aw // hd, 2 * aw // hd, 3 * aw // hd
    tri = (lax.broadcasted_iota(jnp.int32, (tk, tk), 0) > lax.broadcasted_iota(jnp.int32, (tk, tk), 1)).astype(BF16)

    m = n_heads * dt
    qm = (q * (hd ** -0.5 * LOG2E)).transpose(0, 2, 1, 3).reshape(db, m, hd).astype(BF16)
    row_h = jnp.arange(m) // dt
    row_q = jnp.arange(m) % dt
    col_h = jnp.arange(tb) % n_heads
    col_t = jnp.arange(tb) // n_heads
    same_head = row_h[:, None] == col_h[None, :]
    bias_rows = jnp.repeat(sb_bias * LOG2E, dt)[:, None]
    bias_past = jnp.where(same_head, bias_rows, MASKED_LOGIT)
    bias_new = jnp.where(same_head & (col_t[None, :] < row_q[:, None]), bias_rows, MASKED_LOGIT)
    kn = k_new.reshape(db, dt * n_heads, hd)
    vn = v_new.reshape(db, dt * n_heads, hd)

    def sample_step(bi, h, i):
        n = jnp.minimum((bi * grid[1] + h) * grid[2] + i, total - 1)
        return n // n_steps, lax.rem(n, n_steps)

    def seq_block(bi, h, i, pt):
        return (sample_step(bi, h, i)[0], 0, 0)

    def page_spec(r):
        def index(bi, h, i, pt):
            seq, st = sample_step(bi, h, i)
            return (layer, pt[seq * n_pages + (n_pages - (st + 1) * PAGES_PER_STEP + r)], 0, 0, 0)
        return pl.BlockSpec((None, None, PAGE_SIZE, n_heads, hd), index)

    grid_spec = pltpu.PrefetchScalarGridSpec(
        num_scalar_prefetch=1,
        grid=grid,
        in_specs=[
            pl.BlockSpec(memory_space=pltpu.SMEM),
            pl.BlockSpec((None, tq, hd), lambda bi, h, i, pt: (bi, i, qo + h)),
            pl.BlockSpec((None, s, hd), lambda bi, h, i, pt: (bi, 0, ko + h)),
            pl.BlockSpec((None, s, hd), lambda bi, h, i, pt: (bi, 0, vo + h)),
            pl.BlockSpec((nm, hd), lambda bi, h, i, pt: (0, ko + h)),
            pl.BlockSpec((nm, hd), lambda bi, h, i, pt: (0, vo + h)),
            pl.BlockSpec((tk, tk), lambda bi, h, i, pt: (0, 0)),
            pl.BlockSpec((None, m, hd), seq_block),
            pl.BlockSpec((m, tb), lambda bi, h, i, pt: (0, 0)),
            pl.BlockSpec((m, tb), lambda bi, h, i, pt: (0, 0)),
            pl.BlockSpec((None, dt * n_heads, hd), seq_block),
            pl.BlockSpec((None, dt * n_heads, hd), seq_block),
        ] + [page_spec(r) for r in range(PAGES_PER_STEP)] * 2,
        out_specs=[pl.BlockSpec((None, tq, hd), lambda bi, h, i, pt: (bi, i, h)),
                   pl.BlockSpec((None, m, hd), seq_block)],
        scratch_shapes=[pltpu.VMEM((s, hd), BF16), pltpu.VMEM((s, hd), BF16),
                        pltpu.VMEM((nm, hd), BF16), pltpu.VMEM((nm, hd), BF16),
                        pltpu.VMEM((m, 1), F32), pltpu.VMEM((m, hd), F32)],
    )
    o_p, o_s = pl.pallas_call(
        functools.partial(_attn_kernel, tk=tk, hd=hd, grid=grid, n_steps=n_steps, n_seq=db),
        grid_spec=grid_spec,
        out_shape=[jax.ShapeDtypeStruct((b, s, aw), BF16), jax.ShapeDtypeStruct((db, m, hd), F32)],
        compiler_params=_cparams(("arbitrary", "arbitrary", "arbitrary")),
        name="attention",
    )(page_table.reshape(-1), sb_bias, p3, p3, p3, pm, pm, tri, qm, bias_past, bias_new, kn, vn,
      *([cache_k] * PAGES_PER_STEP), *([cache_v] * PAGES_PER_STEP))
    return o_p, o_s.reshape(db, n_heads, dt, hd).transpose(0, 2, 1, 3)


def _sample_start(q_ref, bias_new_ref, kn_ref, vn_ref, tri_ref, c_ref, acc_ref):
    tri = tri_ref[...]
    zpad = jnp.zeros((tri.shape[0] - kn_ref.shape[0], kn_ref.shape[1]), BF16)
    kn = jnp.concatenate([kn_ref[...].astype(BF16), zpad], axis=0)
    vn = jnp.concatenate([vn_ref[...].astype(BF16), zpad], axis=0)
    c_ref[...], acc_ref[...] = _sb_chunk(q_ref[...], kn, vn, tri, bias_new_ref[...])


def _sample_step(active, q_ref, bias_past_ref, tri_ref, page_refs, c_ref, acc_ref):
    def flat(ref):
        return ref[...].reshape(-1, ref.shape[-1]).astype(BF16)

    kc = jnp.concatenate([flat(page_refs[r]) for r in range(PAGES_PER_STEP)], axis=0)
    vc = jnp.concatenate([flat(page_refs[PAGES_PER_STEP + r]) for r in range(PAGES_PER_STEP)], axis=0)
    c, acc = _sb_fold(c_ref[...], acc_ref[...], *_sb_chunk(q_ref[...], kc, vc, tri_ref[...], bias_past_ref[...]))
    if active is not None:
        c = jnp.where(active, c, c_ref[...])
        acc = jnp.where(active, acc, acc_ref[...])
    c_ref[...] = c
    acc_ref[...] = acc


def _attn_kernel(pt_ref, bias_ref, q_ref, k_ref, v_ref, km_ref, vm_ref, tri_ref,
                 sq_ref, bias_past_ref, bias_new_ref, kn_ref, vn_ref, *rest, tk, hd, grid, n_steps, n_seq):
    page_refs = rest[:2 * PAGES_PER_STEP]
    o_ref, so_ref, kb_ref, vb_ref, kmb_ref, vmb_ref, c_ref, acc_ref = rest[2 * PAGES_PER_STEP:]
    h = pl.program_id(1)
    i = pl.program_id(2)
    n = (pl.program_id(0) * grid[1] + h) * grid[2] + i
    total = n_seq * n_steps
    active = None if total == grid[0] * grid[1] * grid[2] else n < total
    st = lax.rem(jnp.minimum(n, total - 1), n_steps)

    @pl.when(i == 0)
    def _():
        kb_ref[...] = k_ref[...].astype(BF16)
        vb_ref[...] = v_ref[...].astype(BF16)
        kmb_ref[...] = km_ref[...].astype(BF16)
        vmb_ref[...] = vm_ref[...].astype(BF16)

    @pl.when(st == 0 if active is None else jnp.logical_and(st == 0, active))
    def _():
        _sample_start(sq_ref, bias_new_ref, kn_ref, vn_ref, tri_ref, c_ref, acc_ref)

    _sample_step(active, sq_ref, bias_past_ref, tri_ref, page_refs, c_ref, acc_ref)
    _prompt_tile(h, i, bias_ref, q_ref, tri_ref, o_ref, kb_ref, vb_ref, kmb_ref, vmb_ref, tk, hd)

    @pl.when(st == n_steps - 1)
    def _():
        so_ref[...] = acc_ref[...]


def _merge_kernel(x_ref, a_ref, o_ref, gp_ref, ga_ref, wbp_ref, wba_ref, wo_ref, y_ref):
    bp = jnp.dot(a_ref[...].astype(BF16), wbp_ref[...], preferred_element_type=F32)
    ba = jnp.dot(o_ref[...].astype(BF16), wba_ref[...], preferred_element_type=F32)
    m = jax.nn.sigmoid(gp_ref[...]) * bp + jax.nn.sigmoid(ga_ref[...]) * ba
    y_ref[...] = x_ref[...] + jnp.dot(m.astype(BF16), wo_ref[...], preferred_element_type=F32)


def _merge(x, a, o, p, w_bp, w_ba, w_out, gate_block):
    t, d = x.shape
    pw, aw = a.shape[1], o.shape[1]
    tm = min(t, 256)
    const = functools.partial(pl.BlockSpec, pipeline_mode=pl.Buffered(1))
    return pl.pallas_call(
        _merge_kernel,
        grid=(pl.cdiv(t, tm),),
        in_specs=[
            pl.BlockSpec((tm, d), lambda i: (i, 0)),
            pl.BlockSpec((tm, pw), lambda i: (i, 0)),
            pl.BlockSpec((tm, aw), lambda i: (i, 0)),
            pl.BlockSpec((tm, d), lambda i: (i, gate_block)),
            pl.BlockSpec((tm, d), lambda i: (i, gate_block + 1)),
            const((pw, d), lambda i: (0, 0)),
            const((aw, d), lambda i: (0, 0)),
            const((d, d), lambda i: (0, 0)),
        ],
        out_specs=pl.BlockSpec((tm, d), lambda i: (i, 0)),
        out_shape=jax.ShapeDtypeStruct((t, d), F32),
        compiler_params=_cparams(("parallel",)),
        name="merge",
    )(x, a, o, p, p, w_bp, w_ba, w_out)


def kernel(x_prompt, x_sample, cache_k, cache_v, state_pool, page_table, meta_tokens, g_ffn1, w_ffn1_gu, w_ffn1_down, g_mix, w_in, g_q, g_k, sb_bias, w_pool, pool_scale, w_branch_pool, w_branch_attn, w_out, g_ffn2, w_ffn2_gu, w_ffn2_down):
    b, s, d = x_prompt.shape
    db, dt, _ = x_sample.shape
    depth, n_heads = sb_bias.shape
    hd = g_q.shape[1]
    aw = n_heads * hd
    pw = state_pool.shape[-1]
    in_w = w_in.shape[-1]
    n_meta = meta_tokens.shape[0]
    past_len = page_table.shape[1] * PAGE_SIZE
    assert depth == 1, "meta rows are dropped after the mixer; a second layer would need them"
    assert pw == aw and in_w == pw + 3 * aw + 2 * d and n_meta == N_META
    l = 0

    ffn1 = _prep_ffn_weights(w_ffn1_gu[l], w_ffn1_down[l])
    ffn2 = _prep_ffn_weights(w_ffn2_gu[l], w_ffn2_down[l])
    w_in_b = w_in[l].astype(BF16)
    w_bp = w_branch_pool[l].astype(BF16)
    w_ba = w_branch_attn[l].astype(BF16)
    w_o = w_out[l].astype(BF16)
    w_pl = w_pool[l].astype(BF16)
    n_chunks = in_w // aw
    head_gain = jnp.ones((n_chunks, 1, aw), F32)
    head_gain = head_gain.at[1, 0].set(jnp.tile(g_q[l], n_heads)).at[2, 0].set(jnp.tile(g_k[l], n_heads))
    gate_block = (pw + 3 * aw) // d

    xr = x_prompt.reshape(b * s, d)
    xe = jnp.concatenate([meta_tokens.astype(F32), x_sample.reshape(db * dt, d)], axis=0)

    x1r = _ffn_half(xr, g_ffn1[l], *ffn1)
    x1e = _ffn_half(xe, g_ffn1[l], *ffn1)
    pr, kr, vr = _in_proj(x1r, g_mix[l], w_in_b, head_gain, n_heads, hd)
    pe, ke, ve = _in_proj(x1e, g_mix[l], w_in_b, head_gain, n_heads, hd)
    pm = pe[:n_meta]
    ps = pe[n_meta:]
    p3 = pr.reshape(b, s, pr.shape[1])
    k4 = kr.reshape(b, s, n_heads, hd)
    v4 = vr.reshape(b, s, n_heads, hd)

    q_s = ps[:, pw:pw + aw].reshape(db, dt, n_heads, hd)
    k_s = ke[n_meta:].reshape(db, dt, n_heads, hd)
    v_s = ve[n_meta:].reshape(db, dt, n_heads, hd)
    o_p, o_s = _attention(p3, pm, q_s, k_s, v_s, cache_k, cache_v, page_table, l, sb_bias[l])

    zrow = jnp.zeros((1, pw), F32)
    hist_p = jnp.concatenate([zrow, pm[n_meta - POOL_HIST:, :pw]], axis=0)[None]
    a_p = _pool_mixer(p3, hist_p, w_pl, pool_scale[l], n_meta, pw)
    x2r = _merge(x1r, a_p.reshape(b * s, pw), o_p.reshape(b * s, aw), pr, w_bp, w_ba, w_o, gate_block)
    y_prompt = _ffn_half(x2r, g_ffn2[l], *ffn2).reshape(b, s, d)

    u_s = ps[:, :pw].reshape(db, dt, pw)
    rows_s = HIST_ROWS * pl.cdiv(dt, HIST_ROWS)
    u_s_pad = jnp.pad(u_s, ((0, 0), (0, rows_s - dt), (0, 0)))
    hist_s = jnp.concatenate([jnp.zeros((db, 1, pw), F32), state_pool[l]], axis=1)
    a_s = _pool_mixer(u_s_pad, hist_s, w_pl, pool_scale[l], past_len, pw)[:, :dt]
    x1s = x1e[n_meta:]
    x2s = _merge(x1s, a_s.reshape(db * dt, pw), o_s.reshape(db * dt, aw).astype(BF16), ps, w_bp, w_ba, w_o, gate_block)
    y_sample = _ffn_half(x2s, g_ffn2[l], *ffn2).reshape(db, dt, d)

    def with_meta(meta, real):
        meta = jnp.broadcast_to(meta[None, :n_meta], (b, n_meta, n_heads, hd))
        return jnp.concatenate([meta, real], axis=1)[None]

    k_prompt = with_meta(ke, k4)
    v_prompt = with_meta(ve, v4)
    pool_prompt = p3[:, s - POOL_HIST:, :pw][None]
    pool_sample = jnp.concatenate([state_pool[l], u_s], axis=1)[:, -POOL_HIST:][None]
    return (y_prompt, y_sample, k_prompt, v_prompt, pool_prompt,
            k_s[None], v_s[None], pool_sample)
```

```python
import functools

import jax
import jax.numpy as jnp
from jax import lax
from jax.experimental import pallas as pl
from jax.experimental.pallas import tpu as pltpu

F32 = jnp.float32
BF16 = jnp.bfloat16

EPS = 1e-6
N_META = 16
POOL_WINDOWS = (2, 4, 8, 16)
POOL_HIST = max(POOL_WINDOWS) - 1
HIST_ROWS = 16
PAGE_SIZE = 128
PAGES_PER_STEP = 16
FF_CHUNK = 512
ROW_TILE = 512
FFN_ROW_TILE = 1024
MXU_COLS = 256
ATTN_TILE = 256
ATTN_Q_BLOCKS = 4
VMEM_LIMIT = 60 * 1024 * 1024


def _cparams(sem):
    return pltpu.CompilerParams(dimension_semantics=sem, vmem_limit_bytes=VMEM_LIMIT)


def _row_tile(t):
    return t if t <= ROW_TILE else ROW_TILE


def _rms_rows(x, g):
    ms = jnp.mean(x * x, axis=-1, keepdims=True)
    return x * lax.rsqrt(ms + EPS) * g


def _ffn_kernel(x_ref, g_ref, wg_ref, wu_ref, wd_ref, o_ref, xn_ref, *, n_col):
    j = pl.program_id(1)

    @pl.when(j == 0)
    def _():
        x = x_ref[...]
        xn_ref[...] = _rms_rows(x, g_ref[...]).astype(BF16)
        o_ref[...] = x

    xn = xn_ref[...]
    a = jnp.dot(xn, wg_ref[...], preferred_element_type=F32)
    b = jnp.dot(xn, wu_ref[...], preferred_element_type=F32)
    act = (((0.5 * a) * jax.nn.sigmoid(a)) * b).astype(BF16)
    cw = o_ref.shape[1] // n_col
    for n in range(n_col):
        cols = slice(n * cw, (n + 1) * cw)
        o_ref[:, cols] += jnp.dot(act, wd_ref[:, cols], preferred_element_type=F32)


def _ffn_half(x, g, wg, wu, wd):
    t, d = x.shape
    tf = FF_CHUNK
    n_ff = wd.shape[0] // tf
    tm = t if t <= FFN_ROW_TILE else FFN_ROW_TILE
    return pl.pallas_call(
        functools.partial(_ffn_kernel, n_col=4),
        grid=(pl.cdiv(t, tm), n_ff),
        in_specs=[
            pl.BlockSpec((tm, d), lambda i, j: (i, 0)),
            pl.BlockSpec((1, d), lambda i, j: (0, 0)),
            pl.BlockSpec((d, tf), lambda i, j: (0, j)),
            pl.BlockSpec((d, tf), lambda i, j: (0, j)),
            pl.BlockSpec((tf, d), lambda i, j: (j, 0)),
        ],
        out_specs=pl.BlockSpec((tm, d), lambda i, j: (i, 0)),
        out_shape=jax.ShapeDtypeStruct((t, d), F32),
        scratch_shapes=[pltpu.VMEM((tm, d), BF16)],
        compiler_params=_cparams(("parallel", "arbitrary")),
        name="ffn_half",
    )(x, g.reshape(1, d), wg, wu, wd)


def _prep_ffn_weights(w_gu, w_down):
    f = w_gu.shape[1] // 2
    pad = FF_CHUNK * pl.cdiv(f, FF_CHUNK) - f
    wg = jnp.pad(w_gu[:, :f].astype(BF16), ((0, 0), (0, pad)))
    wu = jnp.pad(w_gu[:, f:].astype(BF16), ((0, 0), (0, pad)))
    wd = jnp.pad(w_down.astype(BF16), ((0, pad), (0, 0)))
    return wg, wu, wd


def _inproj_kernel(x_ref, g_ref, w_ref, hg_ref, p_ref, k_ref, v_ref, xn_ref, *, n_heads, hd):
    j = pl.program_id(1)

    @pl.when(j == 0)
    def _():
        xn_ref[...] = _rms_rows(x_ref[...], g_ref[...]).astype(BF16)

    heads = [slice(h * hd, (h + 1) * hd) for h in range(n_heads)]

    def projected_heads():
        per_group = max(1, MXU_COLS // hd)
        for h0 in range(0, n_heads, per_group):
            hs = range(h0, min(h0 + per_group, n_heads))
            r = jnp.dot(xn_ref[...], w_ref[:, hs[0] * hd:(hs[-1] + 1) * hd], preferred_element_type=F32)
            for h in hs:
                yield h, heads[h], r[:, (h - h0) * hd:(h - h0 + 1) * hd]

    @pl.when(j == 1)
    def _():
        for h, cols, ph in projected_heads():
            p_ref[:, cols] = _rms_rows(ph, hg_ref[:, cols])

    @pl.when(j == 2)
    def _():
        k = jnp.concatenate([_rms_rows(ph, hg_ref[:, cols]) for _, cols, ph in projected_heads()], axis=1)
        p_ref[...] = k
        k_ref[...] = pltpu.einshape("t(hd)->thd", k, h=n_heads)

    @pl.when(j == 3)
    def _():
        v = jnp.dot(xn_ref[...], w_ref[...], preferred_element_type=F32)
        p_ref[...] = v
        v_ref[...] = pltpu.einshape("t(hd)->thd", v, h=n_heads)

    @pl.when(jnp.logical_or(j == 0, j > 3))
    def _():
        p_ref[...] = jnp.dot(xn_ref[...], w_ref[...], preferred_element_type=F32)


def _in_proj(x, g_mix, w_in, head_gain, n_heads, hd):
    t, d = x.shape
    in_w = w_in.shape[1]
    tn = n_heads * hd
    tm = t if t <= FFN_ROW_TILE else FFN_ROW_TILE
    kv_spec = pl.BlockSpec((tm, n_heads, hd), lambda i, j: (i, 0, 0))
    kv_shape = jax.ShapeDtypeStruct((t, n_heads, hd), F32)
    return pl.pallas_call(
        functools.partial(_inproj_kernel, n_heads=n_heads, hd=hd),
        grid=(pl.cdiv(t, tm), in_w // tn),
        in_specs=[
            pl.BlockSpec((tm, d), lambda i, j: (i, 0)),
            pl.BlockSpec((1, d), lambda i, j: (0, 0)),
            pl.BlockSpec((d, tn), lambda i, j: (0, j)),
            pl.BlockSpec((None, 1, tn), lambda i, j: (j, 0, 0)),
        ],
        out_specs=[pl.BlockSpec((tm, tn), lambda i, j: (i, j)), kv_spec, kv_spec],
        out_shape=[jax.ShapeDtypeStruct((t, in_w), F32), kv_shape, kv_shape],
        scratch_shapes=[pltpu.VMEM((tm, d), BF16)],
        compiler_params=_cparams(("parallel", "arbitrary")),
        name="in_proj",
    )(x, g_mix.reshape(1, d), w_in, head_gain)


def _pool_kernel(u_ref, halo_ref, hist_ref, wp_ref, ps_ref, o_ref, *, tm, pos0, group):
    i = pl.program_id(1)
    prev = jnp.where(i == 0, hist_ref[...], halo_ref[...])
    pos = pos0 + i * tm + lax.broadcasted_iota(jnp.int32, (tm, 1), 0)
    for gi, w in enumerate(POOL_WINDOWS):
        cols = slice(gi * group, (gi + 1) * group)
        x = jnp.concatenate([prev[:, cols], u_ref[:, cols]], axis=0)
        s = x
        span = 1
        while span < w:
            s = s + pltpu.roll(s, span, axis=0)
            span *= 2
        count = jnp.minimum(w, pos + 1).astype(F32)
        d = s[HIST_ROWS:] / count - x[HIST_ROWS:]
        y = jnp.dot(d.astype(BF16), wp_ref[gi], preferred_element_type=F32)
        o_ref[:, cols] = (y * ps_ref[:, cols]).astype(o_ref.dtype)


def _pool_mixer(p3, hist, w_pool, pool_scale, pos0, pw):
    b, s, _ = p3.shape
    tm = s if s <= ROW_TILE else ROW_TILE
    n_groups = w_pool.shape[0]
    group = pw // n_groups
    halo_blocks = tm // HIST_ROWS
    per_batch = hist.shape[0] != 1
    return pl.pallas_call(
        functools.partial(_pool_kernel, tm=tm, pos0=pos0, group=group),
        grid=(b, s // tm),
        in_specs=[
            pl.BlockSpec((None, tm, pw), lambda bi, i: (bi, i, 0)),
            pl.BlockSpec((None, HIST_ROWS, pw), lambda bi, i: (bi, jnp.maximum(i * halo_blocks - 1, 0), 0)),
            pl.BlockSpec((None, HIST_ROWS, pw), (lambda bi, i: (bi, 0, 0)) if per_batch else (lambda bi, i: (0, 0, 0))),
            pl.BlockSpec((n_groups, group, group), lambda bi, i: (0, 0, 0)),
            pl.BlockSpec((1, pw), lambda bi, i: (0, 0)),
        ],
        out_specs=pl.BlockSpec((None, tm, pw), lambda bi, i: (bi, i, 0)),
        out_shape=jax.ShapeDtypeStruct((b, s, pw), BF16),
        compiler_params=_cparams(("parallel", "arbitrary")),
        name="pool_mixer",
    )(p3, p3, hist, w_pool, pool_scale.reshape(1, pw))


LOG2E = 1.4426950408889634
MASKED_LOGIT = -1e30


EXP2_CLAMP = 126.0


def _neg_log2_stay(z):
    return jnp.maximum(jnp.log2(1.0 + jnp.exp2(jnp.minimum(z, EXP2_CLAMP))), z)


def _sb_block(qb, kb, vb, tri, bias2, mask):
    z = lax.dot_general(qb, kb, (((1,), (1,)), ((), ())), preferred_element_type=F32) + bias2
    nls = _neg_log2_stay(z)
    nls_vis = nls if mask is None else jnp.where(mask, nls, 0.0)
    later = jnp.dot(nls_vis.astype(BF16), tri, preferred_element_type=F32)
    w = jnp.exp2(z - nls - later)
    if mask is not None:
        w = jnp.where(mask, w, 0.0)
    pv = jnp.dot(w.astype(BF16), vb, preferred_element_type=F32)
    csum = later[:, :1] + nls_vis[:, :1]
    return csum, pv


def _sb_chunk(qb, kc, vc, tri, bias_tile):
    m = qb.shape[0]
    tb = tri.shape[0]
    n_sub = kc.shape[0] // tb
    z = lax.dot_general(qb, kc, (((1,), (1,)), ((), ())), preferred_element_type=F32)
    z = z + jnp.concatenate([bias_tile] * n_sub, axis=1)
    nls = _neg_log2_stay(z)
    nls_rows = jnp.concatenate([nls[:, j * tb:(j + 1) * tb] for j in range(n_sub)], axis=0)
    later_rows = jnp.dot(nls_rows.astype(BF16), tri, preferred_element_type=F32)
    sub_sum = later_rows[:, :1] + nls_rows[:, :1]
    parts = []
    tail = jnp.zeros((m, 1), F32)
    for j in reversed(range(n_sub)):
        parts.append(later_rows[j * m:(j + 1) * m] + tail)
        tail = tail + sub_sum[j * m:(j + 1) * m]
    later = jnp.concatenate(parts[::-1], axis=1)
    w = jnp.exp2(z - nls - later)
    pv = jnp.dot(w.astype(BF16), vc, preferred_element_type=F32)
    return tail, pv


def _sb_fold(c, acc, csum, pv):
    return c + csum, acc + jnp.exp2(-c) * pv


def _prompt_tile(h, i, bias_ref, q_ref, tri_ref, o_ref, kb_ref, vb_ref, kmb_ref, vmb_ref, tk, hd, side_work):
    tq = q_ref.shape[0]
    nb = tq // tk
    bias2 = bias_ref[h] * LOG2E
    qb = (q_ref[...] * (hd ** -0.5 * LOG2E)).astype(BF16)
    tri = tri_ref[...]

    def kv(j):
        st = pl.multiple_of(j * tk, tk)
        return kb_ref[pl.ds(st, tk), :], vb_ref[pl.ds(st, tk), :]

    nm = kmb_ref.shape[0]
    zt = lax.dot_general(kmb_ref[...], qb, (((1,), (1,)), ((), ())), preferred_element_type=F32) + bias2
    nlst = _neg_log2_stay(zt)
    later_keys = (lax.broadcasted_iota(jnp.int32, (nm, nm), 0) < lax.broadcasted_iota(jnp.int32, (nm, nm), 1))
    latert = jnp.dot(jnp.where(later_keys, 1.0, 0.0).astype(BF16), nlst.astype(BF16), preferred_element_type=F32)
    wt = jnp.exp2(zt - nlst - latert).astype(BF16)
    pv_meta = lax.dot_general(wt, vmb_ref[...], (((0,), (0,)), ((), ())), preferred_element_type=F32)

    lower = lax.broadcasted_iota(jnp.int32, (tk, tk), 1) < lax.broadcasted_iota(jnp.int32, (tk, tk), 0)
    state = [None] * nb
    for jj in reversed(range(nb)):
        kj, vj = kv(nb * i + jj)
        state[jj] = _sb_block(qb[jj * tk:(jj + 1) * tk], kj, vj, tri, bias2, lower)
        if jj + 1 < nb:
            cs, pv = _sb_block(qb[(jj + 1) * tk:], kj, vj, tri, bias2, None)
            for r in range(jj + 1, nb):
                rows = slice((r - jj - 1) * tk, (r - jj) * tk)
                state[r] = _sb_fold(*state[r], cs[rows], pv[rows])
    c = jnp.concatenate([st[0] for st in state], axis=0)
    acc = jnp.concatenate([st[1] for st in state], axis=0)
    side_work()

    def body(t, carry):
        blocks = [_sb_block(qb, *kv(nb * (i - t) - 1 - u), tri, bias2, None) for u in range(nb)]
        for blk in blocks:
            carry = _sb_fold(*carry, *blk)
        return carry

    c, acc = lax.fori_loop(0, i, body, (c, acc))
    o_ref[...] = (acc + jnp.exp2(-c) * pv_meta).astype(o_ref.dtype)


def _attention(p3, pm, q, k_new, v_new, cache_k, cache_v, page_table, layer, sb_bias):
    b, s, _ = p3.shape
    db, dt, n_heads, hd = q.shape
    aw = n_heads * hd
    tk = ATTN_TILE
    tq = ATTN_Q_BLOCKS * tk
    nm = pm.shape[0]
    n_pages = page_table.shape[1]
    tb = tk
    assert s % tq == 0 and tb == 2 * PAGE_SIZE
    assert n_pages % PAGES_PER_STEP == 0 and tb % n_heads == 0 and dt * n_heads <= tb
    n_steps = n_pages // PAGES_PER_STEP
    grid = (b, n_heads, s // tq)
    total = db * n_steps
    assert total <= grid[0] * grid[1] * grid[2], "more sample steps than prompt tiles to pair them with"
    qo, ko, vo = aw // hd, 2 * aw // hd, 3 * aw // hd
    tri = (lax.broadcasted_iota(jnp.int32, (tk, tk), 0) > lax.broadcasted_iota(jnp.int32, (tk, tk), 1)).astype(BF16)

    m = n_heads * dt
    qm = (q * (hd ** -0.5 * LOG2E)).transpose(0, 2, 1, 3).reshape(db, m, hd).astype(BF16)
    row_h = jnp.arange(m) // dt
    row_q = jnp.arange(m) % dt
    col_h = jnp.arange(tb) % n_heads
    col_t = jnp.arange(tb) // n_heads
    same_head = row_h[:, None] == col_h[None, :]
    bias_rows = jnp.repeat(sb_bias * LOG2E, dt)[:, None]
    bias_past = jnp.where(same_head, bias_rows, MASKED_LOGIT)
    bias_new = jnp.where(same_head & (col_t[None, :] < row_q[:, None]), bias_rows, MASKED_LOGIT)
    kn = k_new.reshape(db, dt * n_heads, hd)
    vn = v_new.reshape(db, dt * n_heads, hd)

    def sample_step(bi, h, i):
        n = jnp.minimum((bi * grid[1] + h) * grid[2] + i, total - 1)
        return n // n_steps, lax.rem(n, n_steps)

    def seq_block(bi, h, i, pt):
        return (sample_step(bi, h, i)[0], 0, 0)

    def page_spec(r):
        def index(bi, h, i, pt):
            seq, st = sample_step(bi, h, i)
            return (layer, pt[seq * n_pages + (n_pages - (st + 1) * PAGES_PER_STEP + r)], 0, 0, 0)
        return pl.BlockSpec((None, None, PAGE_SIZE, n_heads, hd), index)

    grid_spec = pltpu.PrefetchScalarGridSpec(
        num_scalar_prefetch=1,
        grid=grid,
        in_specs=[
            pl.BlockSpec(memory_space=pltpu.SMEM),
            pl.BlockSpec((None, tq, hd), lambda bi, h, i, pt: (bi, i, qo + h)),
            pl.BlockSpec((None, s, hd), lambda bi, h, i, pt: (bi, 0, ko + h)),
            pl.BlockSpec((None, s, hd), lambda bi, h, i, pt: (bi, 0, vo + h)),
            pl.BlockSpec((nm, hd), lambda bi, h, i, pt: (0, ko + h)),
            pl.BlockSpec((nm, hd), lambda bi, h, i, pt: (0, vo + h)),
            pl.BlockSpec((tk, tk), lambda bi, h, i, pt: (0, 0)),
            pl.BlockSpec((None, m, hd), seq_block),
            pl.BlockSpec((m, tb), lambda bi, h, i, pt: (0, 0)),
            pl.BlockSpec((m, tb), lambda bi, h, i, pt: (0, 0)),
            pl.BlockSpec((None, dt * n_heads, hd), seq_block),
            pl.BlockSpec((None, dt * n_heads, hd), seq_block),
        ] + [page_spec(r) for r in range(PAGES_PER_STEP)] * 2,
        out_specs=[pl.BlockSpec((None, tq, hd), lambda bi, h, i, pt: (bi, i, h)),
                   pl.BlockSpec((None, m, hd), seq_block)],
        scratch_shapes=[pltpu.VMEM((s, hd), BF16), pltpu.VMEM((s, hd), BF16),
                        pltpu.VMEM((nm, hd), BF16), pltpu.VMEM((nm, hd), BF16),
                        pltpu.VMEM((m, 1), F32), pltpu.VMEM((m, hd), F32)],
    )
    o_p, o_s = pl.pallas_call(
        functools.partial(_attn_kernel, tk=tk, hd=hd, grid=grid, n_steps=n_steps, n_seq=db),
        grid_spec=grid_spec,
        out_shape=[jax.ShapeDtypeStruct((b, s, aw), BF16), jax.ShapeDtypeStruct((db, m, hd), F32)],
        compiler_params=_cparams(("arbitrary", "arbitrary", "arbitrary")),
        name="attention",
    )(page_table.reshape(-1), sb_bias, p3, p3, p3, pm, pm, tri, qm, bias_past, bias_new, kn, vn,
      *([cache_k] * PAGES_PER_STEP), *([cache_v] * PAGES_PER_STEP))
    return o_p, o_s.reshape(db, n_heads, dt, hd).transpose(0, 2, 1, 3)


def _sample_start(q_ref, bias_new_ref, kn_ref, vn_ref, tri_ref, c_ref, acc_ref):
    tri = tri_ref[...]
    zpad = jnp.zeros((tri.shape[0] - kn_ref.shape[0], kn_ref.shape[1]), BF16)
    kn = jnp.concatenate([kn_ref[...].astype(BF16), zpad], axis=0)
    vn = jnp.concatenate([vn_ref[...].astype(BF16), zpad], axis=0)
    c_ref[...], acc_ref[...] = _sb_chunk(q_ref[...], kn, vn, tri, bias_new_ref[...])


def _sample_step(active, q_ref, bias_past_ref, tri_ref, page_refs, c_ref, acc_ref):
    def flat(ref):
        return ref[...].reshape(-1, ref.shape[-1]).astype(BF16)

    kc = jnp.concatenate([flat(page_refs[r]) for r in range(PAGES_PER_STEP)], axis=0)
    vc = jnp.concatenate([flat(page_refs[PAGES_PER_STEP + r]) for r in range(PAGES_PER_STEP)], axis=0)
    c, acc = _sb_fold(c_ref[...], acc_ref[...], *_sb_chunk(q_ref[...], kc, vc, tri_ref[...], bias_past_ref[...]))
    if active is not None:
        c = jnp.where(active, c, c_ref[...])
        acc = jnp.where(active, acc, acc_ref[...])
    c_ref[...] = c
    acc_ref[...] = acc


def _attn_kernel(pt_ref, bias_ref, q_ref, k_ref, v_ref, km_ref, vm_ref, tri_ref,
                 sq_ref, bias_past_ref, bias_new_ref, kn_ref, vn_ref, *rest, tk, hd, grid, n_steps, n_seq):
    page_refs = rest[:2 * PAGES_PER_STEP]
    o_ref, so_ref, kb_ref, vb_ref, kmb_ref, vmb_ref, c_ref, acc_ref = rest[2 * PAGES_PER_STEP:]
    h = pl.program_id(1)
    i = pl.program_id(2)
    n = (pl.program_id(0) * grid[1] + h) * grid[2] + i
    total = n_seq * n_steps
    active = None if total == grid[0] * grid[1] * grid[2] else n < total
    st = lax.rem(jnp.minimum(n, total - 1), n_steps)

    @pl.when(i == 0)
    def _():
        kb_ref[...] = k_ref[...].astype(BF16)
        vb_ref[...] = v_ref[...].astype(BF16)
        kmb_ref[...] = km_ref[...].astype(BF16)
        vmb_ref[...] = vm_ref[...].astype(BF16)

    @pl.when(st == 0 if active is None else jnp.logical_and(st == 0, active))
    def _():
        _sample_start(sq_ref, bias_new_ref, kn_ref, vn_ref, tri_ref, c_ref, acc_ref)

    _prompt_tile(h, i, bias_ref, q_ref, tri_ref, o_ref, kb_ref, vb_ref, kmb_ref, vmb_ref, tk, hd,
                 functools.partial(_sample_step, active, sq_ref, bias_past_ref, tri_ref, page_refs, c_ref, acc_ref))

    @pl.when(st == n_steps - 1)
    def _():
        so_ref[...] = acc_ref[...]


def _merge_kernel(x_ref, a_ref, o_ref, gp_ref, ga_ref, wbp_ref, wba_ref, wo_ref, y_ref):
    bp = jnp.dot(a_ref[...].astype(BF16), wbp_ref[...], preferred_element_type=F32)
    ba = jnp.dot(o_ref[...].astype(BF16), wba_ref[...], preferred_element_type=F32)
    m = jax.nn.sigmoid(gp_ref[...]) * bp + jax.nn.sigmoid(ga_ref[...]) * ba
    y_ref[...] = x_ref[...] + jnp.dot(m.astype(BF16), wo_ref[...], preferred_element_type=F32)


def _merge(x, a, o, p, w_bp, w_ba, w_out, gate_block):
    t, d = x.shape
    pw, aw = a.shape[1], o.shape[1]
    tm = _row_tile(t)
    const = functools.partial(pl.BlockSpec, pipeline_mode=pl.Buffered(1))
    return pl.pallas_call(
        _merge_kernel,
        grid=(pl.cdiv(t, tm),),
        in_specs=[
            pl.BlockSpec((tm, d), lambda i: (i, 0)),
            pl.BlockSpec((tm, pw), lambda i: (i, 0)),
            pl.BlockSpec((tm, aw), lambda i: (i, 0)),
            pl.BlockSpec((tm, d), lambda i: (i, gate_block)),
            pl.BlockSpec((tm, d), lambda i: (i, gate_block + 1)),
            const((pw, d), lambda i: (0, 0)),
            const((aw, d), lambda i: (0, 0)),
            const((d, d), lambda i: (0, 0)),
        ],
        out_specs=pl.BlockSpec((tm, d), lambda i: (i, 0)),
        out_shape=jax.ShapeDtypeStruct((t, d), F32),
        compiler_params=_cparams(("parallel",)),
        name="merge",
    )(x, a, o, p, p, w_bp, w_ba, w_out)


def kernel(x_prompt, x_sample, cache_k, cache_v, state_pool, page_table, meta_tokens, g_ffn1, w_ffn1_gu, w_ffn1_down, g_mix, w_in, g_q, g_k, sb_bias, w_pool, pool_scale, w_branch_pool, w_branch_attn, w_out, g_ffn2, w_ffn2_gu, w_ffn2_down):
    b, s, d = x_prompt.shape
    db, dt, _ = x_sample.shape
    depth, n_heads = sb_bias.shape
    hd = g_q.shape[1]
    aw = n_heads * hd
    pw = state_pool.shape[-1]
    in_w = w_in.shape[-1]
    n_meta = meta_tokens.shape[0]
    past_len = page_table.shape[1] * PAGE_SIZE
    assert depth == 1, "meta rows are dropped after the mixer; a second layer would need them"
    assert pw == aw and in_w == pw + 3 * aw + 2 * d and n_meta == N_META
    l = 0

    ffn1 = _prep_ffn_weights(w_ffn1_gu[l], w_ffn1_down[l])
    ffn2 = _prep_ffn_weights(w_ffn2_gu[l], w_ffn2_down[l])
    w_in_b = w_in[l].astype(BF16)
    w_bp = w_branch_pool[l].astype(BF16)
    w_ba = w_branch_attn[l].astype(BF16)
    w_o = w_out[l].astype(BF16)
    w_pl = w_pool[l].astype(BF16)
    n_chunks = in_w // aw
    head_gain = jnp.ones((n_chunks, 1, aw), F32)
    head_gain = head_gain.at[1, 0].set(jnp.tile(g_q[l], n_heads)).at[2, 0].set(jnp.tile(g_k[l], n_heads))
    gate_block = (pw + 3 * aw) // d

    xr = x_prompt.reshape(b * s, d)
    xe = jnp.concatenate([meta_tokens.astype(F32), x_sample.reshape(db * dt, d)], axis=0)

    x1r = _ffn_half(xr, g_ffn1[l], *ffn1)
    x1e = _ffn_half(xe, g_ffn1[l], *ffn1)
    pr, kr, vr = _in_proj(x1r, g_mix[l], w_in_b, head_gain, n_heads, hd)
    pe, ke, ve = _in_proj(x1e, g_mix[l], w_in_b, head_gain, n_heads, hd)
    pm = pe[:n_meta]
    ps = pe[n_meta:]
    p3 = pr.reshape(b, s, pr.shape[1])
    k4 = kr.reshape(b, s, n_heads, hd)
    v4 = vr.reshape(b, s, n_heads, hd)

    q_s = ps[:, pw:pw + aw].reshape(db, dt, n_heads, hd)
    k_s = ke[n_meta:].reshape(db, dt, n_heads, hd)
    v_s = ve[n_meta:].reshape(db, dt, n_heads, hd)
    o_p, o_s = _attention(p3, pm, q_s, k_s, v_s, cache_k, cache_v, page_table, l, sb_bias[l])

    zrow = jnp.zeros((1, pw), F32)
    hist_p = jnp.concatenate([zrow, pm[n_meta - POOL_HIST:, :pw]], axis=0)[None]
    a_p = _pool_mixer(p3, hist_p, w_pl, pool_scale[l], n_meta, pw)
    x2r = _merge(x1r, a_p.reshape(b * s, pw), o_p.reshape(b * s, aw), pr, w_bp, w_ba, w_o, gate_block)
    y_prompt = _ffn_half(x2r, g_ffn2[l], *ffn2).reshape(b, s, d)

    u_s = ps[:, :pw].reshape(db, dt, pw)
    rows_s = HIST_ROWS * pl.cdiv(dt, HIST_ROWS)
    u_s_pad = jnp.pad(u_s, ((0, 0), (0, rows_s - dt), (0, 0)))
    hist_s = jnp.concatenate([jnp.zeros((db, 1, pw), F32), state_pool[l]], axis=1)
    a_s = _pool_mixer(u_s_pad, hist_s, w_pl, pool_scale[l], past_len, pw)[:, :dt]
    x1s = x1e[n_meta:]
    x2s = _merge(x1s, a_s.reshape(db * dt, pw), o_s.reshape(db * dt, aw).astype(BF16), ps, w_bp, w_ba, w_o, gate_block)
    y_sample = _ffn_half(x2s, g_ffn2[l], *ffn2).reshape(db, dt, d)

    def with_meta(meta, real):
        meta = jnp.broadcast_to(meta[None, :n_meta], (b, n_meta, n_heads, hd))
        return jnp.concatenate([meta, real], axis=1)[None]

    k_prompt = with_meta(ke, k4)
    v_prompt = with_meta(ve, v4)
    pool_prompt = p3[:, s - POOL_HIST:, :pw][None]
    pool_sample = jnp.concatenate([state_pool[l], u_s], axis=1)[:, -POOL_HIST:][None]
    return (y_prompt, y_sample, k_prompt, v_prompt, pool_prompt,
            k_s[None], v_s[None], pool_sample)
```

```python
import functools

import jax
import jax.numpy as jnp
from jax import lax
from jax.experimental import pallas as pl
from jax.experimental.pallas import tpu as pltpu

F32 = jnp.float32
BF16 = jnp.bfloat16

EPS = 1e-6
N_META = 16
POOL_WINDOWS = (2, 4, 8, 16)
POOL_HIST = max(POOL_WINDOWS) - 1
HIST_ROWS = 16
PAGE_SIZE = 128
PAGES_PER_STEP = 16
FF_CHUNK = 512
ROW_TILE = 512
FFN_ROW_TILE = 1024
MXU_COLS = 256
ATTN_TILE = 256
ATTN_Q_BLOCKS = 4
VMEM_LIMIT = 60 * 1024 * 1024


def _cparams(sem):
    return pltpu.CompilerParams(dimension_semantics=sem, vmem_limit_bytes=VMEM_LIMIT)


def _row_tile(t):
    return t if t <= ROW_TILE else ROW_TILE


def _rms_rows(x, g):
    ms = jnp.mean(x * x, axis=-1, keepdims=True)
    return x * lax.rsqrt(ms + EPS) * g


def _ffn_kernel(x_ref, g_ref, wg_ref, wu_ref, wd_ref, o_ref, xn_ref, *, n_col):
    j = pl.program_id(1)

    @pl.when(j == 0)
    def _():
        x = x_ref[...]
        xn_ref[...] = _rms_rows(x, g_ref[...]).astype(BF16)
        o_ref[...] = x

    xn = xn_ref[...]
    a = jnp.dot(xn, wg_ref[...], preferred_element_type=F32)
    b = jnp.dot(xn, wu_ref[...], preferred_element_type=F32)
    act = (((0.5 * a) * jax.nn.sigmoid(a)) * b).astype(BF16)
    cw = o_ref.shape[1] // n_col
    for n in range(n_col):
        cols = slice(n * cw, (n + 1) * cw)
        o_ref[:, cols] += jnp.dot(act, wd_ref[:, cols], preferred_element_type=F32)


def _ffn_half(x, g, wg, wu, wd):
    t, d = x.shape
    tf = FF_CHUNK
    n_ff = wd.shape[0] // tf
    tm = t if t <= FFN_ROW_TILE else FFN_ROW_TILE
    return pl.pallas_call(
        functools.partial(_ffn_kernel, n_col=4),
        grid=(pl.cdiv(t, tm), n_ff),
        in_specs=[
            pl.BlockSpec((tm, d), lambda i, j: (i, 0)),
            pl.BlockSpec((1, d), lambda i, j: (0, 0)),
            pl.BlockSpec((d, tf), lambda i, j: (0, j)),
            pl.BlockSpec((d, tf), lambda i, j: (0, j)),
            pl.BlockSpec((tf, d), lambda i, j: (j, 0)),
        ],
        out_specs=pl.BlockSpec((tm, d), lambda i, j: (i, 0)),
        out_shape=jax.ShapeDtypeStruct((t, d), F32),
        scratch_shapes=[pltpu.VMEM((tm, d), BF16)],
        compiler_params=_cparams(("parallel", "arbitrary")),
        name="ffn_half",
    )(x, g.reshape(1, d), wg, wu, wd)


def _prep_ffn_weights(w_gu, w_down):
    f = w_gu.shape[1] // 2
    pad = FF_CHUNK * pl.cdiv(f, FF_CHUNK) - f
    wg = jnp.pad(w_gu[:, :f].astype(BF16), ((0, 0), (0, pad)))
    wu = jnp.pad(w_gu[:, f:].astype(BF16), ((0, 0), (0, pad)))
    wd = jnp.pad(w_down.astype(BF16), ((0, pad), (0, 0)))
    return wg, wu, wd


def _inproj_kernel(x_ref, g_ref, w_ref, hg_ref, p_ref, k_ref, v_ref, xn_ref, *, n_heads, hd):
    j = pl.program_id(1)

    @pl.when(j == 0)
    def _():
        xn_ref[...] = _rms_rows(x_ref[...], g_ref[...]).astype(BF16)

    heads = [slice(h * hd, (h + 1) * hd) for h in range(n_heads)]

    def projected_heads():
        per_group = max(1, MXU_COLS // hd)
        for h0 in range(0, n_heads, per_group):
            hs = range(h0, min(h0 + per_group, n_heads))
            r = jnp.dot(xn_ref[...], w_ref[:, hs[0] * hd:(hs[-1] + 1) * hd], preferred_element_type=F32)
            for h in hs:
                yield h, heads[h], r[:, (h - h0) * hd:(h - h0 + 1) * hd]

    @pl.when(j == 1)
    def _():
        for h, cols, ph in projected_heads():
            p_ref[:, cols] = _rms_rows(ph, hg_ref[:, cols])

    @pl.when(j == 2)
    def _():
        k = jnp.concatenate([_rms_rows(ph, hg_ref[:, cols]) for _, cols, ph in projected_heads()], axis=1)
        p_ref[...] = k
        k_ref[...] = pltpu.einshape("t(hd)->thd", k, h=n_heads).reshape(k_ref.shape)

    @pl.when(j == 3)
    def _():
        v = jnp.dot(xn_ref[...], w_ref[...], preferred_element_type=F32)
        p_ref[...] = v
        v_ref[...] = pltpu.einshape("t(hd)->thd", v, h=n_heads).reshape(v_ref.shape)

    @pl.when(jnp.logical_or(j == 0, j > 3))
    def _():
        p_ref[...] = jnp.dot(xn_ref[...], w_ref[...], preferred_element_type=F32)


def _in_proj(x, g_mix, w_in, head_gain, n_heads, hd, seq_layout=None):
    t, d = x.shape
    in_w = w_in.shape[1]
    tn = n_heads * hd
    tm = t if t <= FFN_ROW_TILE else FFN_ROW_TILE
    if seq_layout is None:
        kv_spec = pl.BlockSpec((tm, n_heads, hd), lambda i, j: (i, 0, 0))
        kv_shape = jax.ShapeDtypeStruct((t, n_heads, hd), F32)
    else:
        n_seq, lead = seq_layout
        rows = t // n_seq
        assert rows % tm == 0
        per_seq = rows // tm
        kv_spec = pl.BlockSpec((pl.Element(1), pl.Element(tm), pl.Element(n_heads), pl.Element(hd)),
                               lambda i, j: (i // per_seq, lead + lax.rem(i, per_seq) * tm, 0, 0))
        kv_shape = jax.ShapeDtypeStruct((n_seq, lead + rows, n_heads, hd), F32)
    return pl.pallas_call(
        functools.partial(_inproj_kernel, n_heads=n_heads, hd=hd),
        grid=(pl.cdiv(t, tm), in_w // tn),
        in_specs=[
            pl.BlockSpec((tm, d), lambda i, j: (i, 0)),
            pl.BlockSpec((1, d), lambda i, j: (0, 0)),
            pl.BlockSpec((d, tn), lambda i, j: (0, j)),
            pl.BlockSpec((None, 1, tn), lambda i, j: (j, 0, 0)),
        ],
        out_specs=[pl.BlockSpec((tm, tn), lambda i, j: (i, j)), kv_spec, kv_spec],
        out_shape=[jax.ShapeDtypeStruct((t, in_w), F32), kv_shape, kv_shape],
        scratch_shapes=[pltpu.VMEM((tm, d), BF16)],
        compiler_params=_cparams(("parallel", "arbitrary")),
        name="in_proj",
    )(x, g_mix.reshape(1, d), w_in, head_gain)


def _fill_lead_kernel(lead_ref, seq_hbm_ref, o_ref):
    del seq_hbm_ref
    o_ref[...] = lead_ref[...]


def _fill_lead(seqs, lead_rows):
    n_seq, _, n_heads, hd = seqs.shape
    lead = lead_rows.shape[0]
    return pl.pallas_call(
        _fill_lead_kernel,
        grid=(n_seq,),
        in_specs=[pl.BlockSpec((lead, n_heads, hd), lambda bi: (0, 0, 0)),
                  pl.BlockSpec(memory_space=pl.ANY)],
        out_specs=pl.BlockSpec((None, lead, n_heads, hd), lambda bi: (bi, 0, 0, 0)),
        out_shape=jax.ShapeDtypeStruct(seqs.shape, seqs.dtype),
        input_output_aliases={1: 0},
        compiler_params=_cparams(("parallel",)),
        name="fill_lead",
    )(lead_rows, seqs)


def _pool_kernel(u_ref, halo_ref, hist_ref, wp_ref, ps_ref, o_ref, *, tm, pos0, group):
    i = pl.program_id(1)
    prev = jnp.where(i == 0, hist_ref[...], halo_ref[...])
    pos = pos0 + i * tm + lax.broadcasted_iota(jnp.int32, (tm, 1), 0)
    for gi, w in enumerate(POOL_WINDOWS):
        cols = slice(gi * group, (gi + 1) * group)
        x = jnp.concatenate([prev[:, cols], u_ref[:, cols]], axis=0)
        s = x
        span = 1
        while span < w:
            s = s + pltpu.roll(s, span, axis=0)
            span *= 2
        count = jnp.minimum(w, pos + 1).astype(F32)
        d = s[HIST_ROWS:] / count - x[HIST_ROWS:]
        y = jnp.dot(d.astype(BF16), wp_ref[gi], preferred_element_type=F32)
        o_ref[:, cols] = (y * ps_ref[:, cols]).astype(o_ref.dtype)


def _pool_mixer(p3, hist, w_pool, pool_scale, pos0, pw):
    b, s, _ = p3.shape
    tm = s if s <= ROW_TILE else ROW_TILE
    n_groups = w_pool.shape[0]
    group = pw // n_groups
    halo_blocks = tm // HIST_ROWS
    per_batch = hist.shape[0] != 1
    return pl.pallas_call(
        functools.partial(_pool_kernel, tm=tm, pos0=pos0, group=group),
        grid=(b, s // tm),
        in_specs=[
            pl.BlockSpec((None, tm, pw), lambda bi, i: (bi, i, 0)),
            pl.BlockSpec((None, HIST_ROWS, pw), lambda bi, i: (bi, jnp.maximum(i * halo_blocks - 1, 0), 0)),
            pl.BlockSpec((None, HIST_ROWS, pw), (lambda bi, i: (bi, 0, 0)) if per_batch else (lambda bi, i: (0, 0, 0))),
            pl.BlockSpec((n_groups, group, group), lambda bi, i: (0, 0, 0)),
            pl.BlockSpec((1, pw), lambda bi, i: (0, 0)),
        ],
        out_specs=pl.BlockSpec((None, tm, pw), lambda bi, i: (bi, i, 0)),
        out_shape=jax.ShapeDtypeStruct((b, s, pw), BF16),
        compiler_params=_cparams(("parallel", "arbitrary")),
        name="pool_mixer",
    )(p3, p3, hist, w_pool, pool_scale.reshape(1, pw))


LOG2E = 1.4426950408889634
MASKED_LOGIT = -1e30


EXP2_CLAMP = 126.0


def _neg_log2_stay(z):
    return jnp.maximum(jnp.log2(1.0 + jnp.exp2(jnp.minimum(z, EXP2_CLAMP))), z)


def _sb_block(qb, kb, vb, tri, bias2, mask):
    z = lax.dot_general(qb, kb, (((1,), (1,)), ((), ())), preferred_element_type=F32) + bias2
    nls = _neg_log2_stay(z)
    nls_vis = nls if mask is None else jnp.where(mask, nls, 0.0)
    later = jnp.dot(nls_vis.astype(BF16), tri, preferred_element_type=F32)
    w = jnp.exp2(z - nls - later)
    if mask is not None:
        w = jnp.where(mask, w, 0.0)
    pv = jnp.dot(w.astype(BF16), vb, preferred_element_type=F32)
    csum = later[:, :1] + nls_vis[:, :1]
    return csum, pv


def _sb_chunk(qb, kc, vc, tri, bias_tile):
    m = qb.shape[0]
    tb = tri.shape[0]
    n_sub = kc.shape[0] // tb
    z = lax.dot_general(qb, kc, (((1,), (1,)), ((), ())), preferred_element_type=F32)
    z = z + jnp.concatenate([bias_tile] * n_sub, axis=1)
    nls = _neg_log2_stay(z)
    nls_rows = jnp.concatenate([nls[:, j * tb:(j + 1) * tb] for j in range(n_sub)], axis=0)
    later_rows = jnp.dot(nls_rows.astype(BF16), tri, preferred_element_type=F32)
    sub_sum = later_rows[:, :1] + nls_rows[:, :1]
    parts = []
    tail = jnp.zeros((m, 1), F32)
    for j in reversed(range(n_sub)):
        parts.append(later_rows[j * m:(j + 1) * m] + tail)
        tail = tail + sub_sum[j * m:(j + 1) * m]
    later = jnp.concatenate(parts[::-1], axis=1)
    w = jnp.exp2(z - nls - later)
    pv = jnp.dot(w.astype(BF16), vc, preferred_element_type=F32)
    return tail, pv


def _sb_fold(c, acc, csum, pv):
    return c + csum, acc + jnp.exp2(-c) * pv


def _prompt_tile(h, i, bias_ref, q_ref, tri_ref, o_ref, kb_ref, vb_ref, kmb_ref, vmb_ref, tk, hd, side_work):
    tq = q_ref.shape[0]
    nb = tq // tk
    bias2 = bias_ref[h] * LOG2E
    qb = (q_ref[...] * (hd ** -0.5 * LOG2E)).astype(BF16)
    tri = tri_ref[...]

    def kv(j):
        st = pl.multiple_of(j * tk, tk)
        return kb_ref[pl.ds(st, tk), :], vb_ref[pl.ds(st, tk), :]

    nm = kmb_ref.shape[0]
    zt = lax.dot_general(kmb_ref[...], qb, (((1,), (1,)), ((), ())), preferred_element_type=F32) + bias2
    nlst = _neg_log2_stay(zt)
    later_keys = (lax.broadcasted_iota(jnp.int32, (nm, nm), 0) < lax.broadcasted_iota(jnp.int32, (nm, nm), 1))
    latert = jnp.dot(jnp.where(later_keys, 1.0, 0.0).astype(BF16), nlst.astype(BF16), preferred_element_type=F32)
    wt = jnp.exp2(zt - nlst - latert).astype(BF16)
    pv_meta = lax.dot_general(wt, vmb_ref[...], (((0,), (0,)), ((), ())), preferred_element_type=F32)

    lower = lax.broadcasted_iota(jnp.int32, (tk, tk), 1) < lax.broadcasted_iota(jnp.int32, (tk, tk), 0)
    state = [None] * nb
    for jj in reversed(range(nb)):
        kj, vj = kv(nb * i + jj)
        state[jj] = _sb_block(qb[jj * tk:(jj + 1) * tk], kj, vj, tri, bias2, lower)
        if jj + 1 < nb:
            cs, pv = _sb_block(qb[(jj + 1) * tk:], kj, vj, tri, bias2, None)
            for r in range(jj + 1, nb):
                rows = slice((r - jj - 1) * tk, (r - jj) * tk)
                state[r] = _sb_fold(*state[r], cs[rows], pv[rows])
    c = jnp.concatenate([st[0] for st in state], axis=0)
    acc = jnp.concatenate([st[1] for st in state], axis=0)
    side_work()

    def body(t, carry):
        blocks = [_sb_block(qb, *kv(nb * (i - t) - 1 - u), tri, bias2, None) for u in range(nb)]
        for blk in blocks:
            carry = _sb_fold(*carry, *blk)
        return carry

    c, acc = lax.fori_loop(0, i, body, (c, acc))
    o_ref[...] = (acc + jnp.exp2(-c) * pv_meta).astype(o_ref.dtype)


def _attention(p3, pm, q, k_new, v_new, cache_k, cache_v, page_table, layer, sb_bias):
    b, s, _ = p3.shape
    db, dt, n_heads, hd = q.shape
    aw = n_heads * hd
    tk = ATTN_TILE
    tq = ATTN_Q_BLOCKS * tk
    nm = pm.shape[0]
    n_pages = page_table.shape[1]
    tb = tk
    assert s % tq == 0 and tb == 2 * PAGE_SIZE
    assert n_pages % PAGES_PER_STEP == 0 and tb % n_heads == 0 and dt * n_heads <= tb
    n_steps = n_pages // PAGES_PER_STEP
    grid = (b, n_heads, s // tq)
    total = db * n_steps
    assert total <= grid[0] * grid[1] * grid[2], "more sample steps than prompt tiles to pair them with"
    qo, ko, vo = aw // hd, 2 * aw // hd, 3 * aw // hd
    tri = (lax.broadcasted_iota(jnp.int32, (tk, tk), 0) > lax.broadcasted_iota(jnp.int32, (tk, tk), 1)).astype(BF16)

    m = n_heads * dt
    qm = (q * (hd ** -0.5 * LOG2E)).transpose(0, 2, 1, 3).reshape(db, m, hd).astype(BF16)
    row_h = jnp.arange(m) // dt
    row_q = jnp.arange(m) % dt
    col_h = jnp.arange(tb) % n_heads
    col_t = jnp.arange(tb) // n_heads
    same_head = row_h[:, None] == col_h[None, :]
    bias_rows = jnp.repeat(sb_bias * LOG2E, dt)[:, None]
    bias_past = jnp.where(same_head, bias_rows, MASKED_LOGIT)
    bias_new = jnp.where(same_head & (col_t[None, :] < row_q[:, None]), bias_rows, MASKED_LOGIT)
    kn = k_new.reshape(db, dt * n_heads, hd)
    vn = v_new.reshape(db, dt * n_heads, hd)

    def sample_step(bi, h, i):
        n = jnp.minimum((bi * grid[1] + h) * grid[2] + i, total - 1)
        return n // n_steps, lax.rem(n, n_steps)

    def seq_block(bi, h, i, pt):
        return (sample_step(bi, h, i)[0], 0, 0)

    def page_spec(r):
        def index(bi, h, i, pt):
            seq, st = sample_step(bi, h, i)
            return (layer, pt[seq * n_pages + (n_pages - (st + 1) * PAGES_PER_STEP + r)], 0, 0, 0)
        return pl.BlockSpec((None, None, PAGE_SIZE, n_heads, hd), index)

    grid_spec = pltpu.PrefetchScalarGridSpec(
        num_scalar_prefetch=1,
        grid=grid,
        in_specs=[
            pl.BlockSpec(memory_space=pltpu.SMEM),
            pl.BlockSpec((None, tq, hd), lambda bi, h, i, pt: (bi, i, qo + h)),
            pl.BlockSpec((None, s, hd), lambda bi, h, i, pt: (bi, 0, ko + h)),
            pl.BlockSpec((None, s, hd), lambda bi, h, i, pt: (bi, 0, vo + h)),
            pl.BlockSpec((nm, hd), lambda bi, h, i, pt: (0, ko + h)),
            pl.BlockSpec((nm, hd), lambda bi, h, i, pt: (0, vo + h)),
            pl.BlockSpec((tk, tk), lambda bi, h, i, pt: (0, 0)),
            pl.BlockSpec((None, m, hd), seq_block),
            pl.BlockSpec((m, tb), lambda bi, h, i, pt: (0, 0)),
            pl.BlockSpec((m, tb), lambda bi, h, i, pt: (0, 0)),
            pl.BlockSpec((None, dt * n_heads, hd), seq_block),
            pl.BlockSpec((None, dt * n_heads, hd), seq_block),
        ] + [page_spec(r) for r in range(PAGES_PER_STEP)] * 2,
        out_specs=[pl.BlockSpec((None, tq, hd), lambda bi, h, i, pt: (bi, i, h)),
                   pl.BlockSpec((None, m, hd), seq_block)],
        scratch_shapes=[pltpu.VMEM((s, hd), BF16), pltpu.VMEM((s, hd), BF16),
                        pltpu.VMEM((nm, hd), BF16), pltpu.VMEM((nm, hd), BF16),
                        pltpu.VMEM((m, 1), F32), pltpu.VMEM((m, hd), F32)],
    )
    o_p, o_s = pl.pallas_call(
        functools.partial(_attn_kernel, tk=tk, hd=hd, grid=grid, n_steps=n_steps, n_seq=db),
        grid_spec=grid_spec,
        out_shape=[jax.ShapeDtypeStruct((b, s, aw), BF16), jax.ShapeDtypeStruct((db, m, hd), F32)],
        compiler_params=_cparams(("arbitrary", "arbitrary", "arbitrary")),
        name="attention",
    )(page_table.reshape(-1), sb_bias, p3, p3, p3, pm, pm, tri, qm, bias_past, bias_new, kn, vn,
      *([cache_k] * PAGES_PER_STEP), *([cache_v] * PAGES_PER_STEP))
    return o_p, o_s.reshape(db, n_heads, dt, hd).transpose(0, 2, 1, 3)


def _sample_start(q_ref, bias_new_ref, kn_ref, vn_ref, tri_ref, c_ref, acc_ref):
    tri = tri_ref[...]
    zpad = jnp.zeros((tri.shape[0] - kn_ref.shape[0], kn_ref.shape[1]), BF16)
    kn = jnp.concatenate([kn_ref[...].astype(BF16), zpad], axis=0)
    vn = jnp.concatenate([vn_ref[...].astype(BF16), zpad], axis=0)
    c_ref[...], acc_ref[...] = _sb_chunk(q_ref[...], kn, vn, tri, bias_new_ref[...])


def _sample_step(active, q_ref, bias_past_ref, tri_ref, page_refs, c_ref, acc_ref):
    def flat(ref):
        return ref[...].reshape(-1, ref.shape[-1]).astype(BF16)

    kc = jnp.concatenate([flat(page_refs[r]) for r in range(PAGES_PER_STEP)], axis=0)
    vc = jnp.concatenate([flat(page_refs[PAGES_PER_STEP + r]) for r in range(PAGES_PER_STEP)], axis=0)
    c, acc = _sb_fold(c_ref[...], acc_ref[...], *_sb_chunk(q_ref[...], kc, vc, tri_ref[...], bias_past_ref[...]))
    if active is not None:
        c = jnp.where(active, c, c_ref[...])
        acc = jnp.where(active, acc, acc_ref[...])
    c_ref[...] = c
    acc_ref[...] = acc


def _attn_kernel(pt_ref, bias_ref, q_ref, k_ref, v_ref, km_ref, vm_ref, tri_ref,
                 sq_ref, bias_past_ref, bias_new_ref, kn_ref, vn_ref, *rest, tk, hd, grid, n_steps, n_seq):
    page_refs = rest[:2 * PAGES_PER_STEP]
    o_ref, so_ref, kb_ref, vb_ref, kmb_ref, vmb_ref, c_ref, acc_ref = rest[2 * PAGES_PER_STEP:]
    h = pl.program_id(1)
    i = pl.program_id(2)
    n = (pl.program_id(0) * grid[1] + h) * grid[2] + i
    total = n_seq * n_steps
    active = None if total == grid[0] * grid[1] * grid[2] else n < total
    st = lax.rem(jnp.minimum(n, total - 1), n_steps)

    @pl.when(i == 0)
    def _():
        kb_ref[...] = k_ref[...].astype(BF16)
        vb_ref[...] = v_ref[...].astype(BF16)
        kmb_ref[...] = km_ref[...].astype(BF16)
        vmb_ref[...] = vm_ref[...].astype(BF16)

    @pl.when(st == 0 if active is None else jnp.logical_and(st == 0, active))
    def _():
        _sample_start(sq_ref, bias_new_ref, kn_ref, vn_ref, tri_ref, c_ref, acc_ref)

    _prompt_tile(h, i, bias_ref, q_ref, tri_ref, o_ref, kb_ref, vb_ref, kmb_ref, vmb_ref, tk, hd,
                 functools.partial(_sample_step, active, sq_ref, bias_past_ref, tri_ref, page_refs, c_ref, acc_ref))

    @pl.when(st == n_steps - 1)
    def _():
        so_ref[...] = acc_ref[...]


def _merge_kernel(x_ref, a_ref, o_ref, gp_ref, ga_ref, wbp_ref, wba_ref, wo_ref, y_ref):
    bp = jnp.dot(a_ref[...].astype(BF16), wbp_ref[...], preferred_element_type=F32)
    ba = jnp.dot(o_ref[...].astype(BF16), wba_ref[...], preferred_element_type=F32)
    m = jax.nn.sigmoid(gp_ref[...]) * bp + jax.nn.sigmoid(ga_ref[...]) * ba
    y_ref[...] = x_ref[...] + jnp.dot(m.astype(BF16), wo_ref[...], preferred_element_type=F32)


def _merge(x, a, o, p, w_bp, w_ba, w_out, gate_block):
    t, d = x.shape
    pw, aw = a.shape[1], o.shape[1]
    tm = _row_tile(t)
    const = functools.partial(pl.BlockSpec, pipeline_mode=pl.Buffered(1))
    return pl.pallas_call(
        _merge_kernel,
        grid=(pl.cdiv(t, tm),),
        in_specs=[
            pl.BlockSpec((tm, d), lambda i: (i, 0)),
            pl.BlockSpec((tm, pw), lambda i: (i, 0)),
            pl.BlockSpec((tm, aw), lambda i: (i, 0)),
            pl.BlockSpec((tm, d), lambda i: (i, gate_block)),
            pl.BlockSpec((tm, d), lambda i: (i, gate_block + 1)),
            const((pw, d), lambda i: (0, 0)),
            const((aw, d), lambda i: (0, 0)),
            const((d, d), lambda i: (0, 0)),
        ],
        out_specs=pl.BlockSpec((tm, d), lambda i: (i, 0)),
        out_shape=jax.ShapeDtypeStruct((t, d), F32),
        compiler_params=_cparams(("parallel",)),
        name="merge",
    )(x, a, o, p, p, w_bp, w_ba, w_out)


def kernel(x_prompt, x_sample, cache_k, cache_v, state_pool, page_table, meta_tokens, g_ffn1, w_ffn1_gu, w_ffn1_down, g_mix, w_in, g_q, g_k, sb_bias, w_pool, pool_scale, w_branch_pool, w_branch_attn, w_out, g_ffn2, w_ffn2_gu, w_ffn2_down):
    b, s, d = x_prompt.shape
    db, dt, _ = x_sample.shape
    depth, n_heads = sb_bias.shape
    hd = g_q.shape[1]
    aw = n_heads * hd
    pw = state_pool.shape[-1]
    in_w = w_in.shape[-1]
    n_meta = meta_tokens.shape[0]
    past_len = page_table.shape[1] * PAGE_SIZE
    assert depth == 1, "meta rows are dropped after the mixer; a second layer would need them"
    assert pw == aw and in_w == pw + 3 * aw + 2 * d and n_meta == N_META
    l = 0

    ffn1 = _prep_ffn_weights(w_ffn1_gu[l], w_ffn1_down[l])
    ffn2 = _prep_ffn_weights(w_ffn2_gu[l], w_ffn2_down[l])
    w_in_b = w_in[l].astype(BF16)
    w_bp = w_branch_pool[l].astype(BF16)
    w_ba = w_branch_attn[l].astype(BF16)
    w_o = w_out[l].astype(BF16)
    w_pl = w_pool[l].astype(BF16)
    n_chunks = in_w // aw
    head_gain = jnp.ones((n_chunks, 1, aw), F32)
    head_gain = head_gain.at[1, 0].set(jnp.tile(g_q[l], n_heads)).at[2, 0].set(jnp.tile(g_k[l], n_heads))
    gate_block = (pw + 3 * aw) // d

    xr = x_prompt.reshape(b * s, d)
    xe = jnp.concatenate([meta_tokens.astype(F32), x_sample.reshape(db * dt, d)], axis=0)

    x1r = _ffn_half(xr, g_ffn1[l], *ffn1)
    x1e = _ffn_half(xe, g_ffn1[l], *ffn1)
    pr, kp, vp = _in_proj(x1r, g_mix[l], w_in_b, head_gain, n_heads, hd, seq_layout=(b, n_meta))
    pe, ke, ve = _in_proj(x1e, g_mix[l], w_in_b, head_gain, n_heads, hd)
    pm = pe[:n_meta]
    ps = pe[n_meta:]
    p3 = pr.reshape(b, s, pr.shape[1])

    q_s = ps[:, pw:pw + aw].reshape(db, dt, n_heads, hd)
    k_s = ke[n_meta:].reshape(db, dt, n_heads, hd)
    v_s = ve[n_meta:].reshape(db, dt, n_heads, hd)
    o_p, o_s = _attention(p3, pm, q_s, k_s, v_s, cache_k, cache_v, page_table, l, sb_bias[l])

    zrow = jnp.zeros((1, pw), F32)
    hist_p = jnp.concatenate([zrow, pm[n_meta - POOL_HIST:, :pw]], axis=0)[None]
    a_p = _pool_mixer(p3, hist_p, w_pl, pool_scale[l], n_meta, pw)
    x2r = _merge(x1r, a_p.reshape(b * s, pw), o_p.reshape(b * s, aw), pr, w_bp, w_ba, w_o, gate_block)
    y_prompt = _ffn_half(x2r, g_ffn2[l], *ffn2).reshape(b, s, d)

    u_s = ps[:, :pw].reshape(db, dt, pw)
    rows_s = HIST_ROWS * pl.cdiv(dt, HIST_ROWS)
    u_s_pad = jnp.pad(u_s, ((0, 0), (0, rows_s - dt), (0, 0)))
    hist_s = jnp.concatenate([jnp.zeros((db, 1, pw), F32), state_pool[l]], axis=1)
    a_s = _pool_mixer(u_s_pad, hist_s, w_pl, pool_scale[l], past_len, pw)[:, :dt]
    x1s = x1e[n_meta:]
    x2s = _merge(x1s, a_s.reshape(db * dt, pw), o_s.reshape(db * dt, aw).astype(BF16), ps, w_bp, w_ba, w_o, gate_block)
    y_sample = _ffn_half(x2s, g_ffn2[l], *ffn2).reshape(db, dt, d)

    k_prompt = _fill_lead(kp, ke[:n_meta])[None]
    v_prompt = _fill_lead(vp, ve[:n_meta])[None]
    pool_prompt = p3[:, s - POOL_HIST:, :pw][None]
    pool_sample = jnp.concatenate([state_pool[l], u_s], axis=1)[:, -POOL_HIST:][None]
    return (y_prompt, y_sample, k_prompt, v_prompt, pool_prompt,
            k_s[None], v_s[None], pool_sample)
```

```python
import functools

import jax
import jax.numpy as jnp
from jax import lax
from jax.experimental import pallas as pl
from jax.experimental.pallas import tpu as pltpu

F32 = jnp.float32
BF16 = jnp.bfloat16

EPS = 1e-6
N_META = 16
POOL_WINDOWS = (2, 4, 8, 16)
POOL_HIST = max(POOL_WINDOWS) - 1
HIST_ROWS = 16
PAGE_SIZE = 128
PAGES_PER_STEP = 16
FF_CHUNK = 512
ROW_TILE = 512
FFN_ROW_TILE = 1024
POOL_ROW_TILE = 1024
MXU_COLS = 256
ATTN_TILE = 256
ATTN_Q_BLOCKS = 4
VMEM_LIMIT = 60 * 1024 * 1024


def _cparams(sem):
    return pltpu.CompilerParams(dimension_semantics=sem, vmem_limit_bytes=VMEM_LIMIT)


def _row_tile(t):
    return t if t <= ROW_TILE else ROW_TILE


def _rms_rows(x, g):
    ms = jnp.mean(x * x, axis=-1, keepdims=True)
    return x * lax.rsqrt(ms + EPS) * g


def _ffn_kernel(x_ref, g_ref, wg_ref, wu_ref, wd_ref, o_ref, xn_ref, *, n_col):
    j = pl.program_id(1)

    @pl.when(j == 0)
    def _():
        x = x_ref[...]
        xn_ref[...] = _rms_rows(x, g_ref[...]).astype(BF16)
        o_ref[...] = x

    xn = xn_ref[...]
    a = jnp.dot(xn, wg_ref[...], preferred_element_type=F32)
    b = jnp.dot(xn, wu_ref[...], preferred_element_type=F32)
    act = (((0.5 * a) * jax.nn.sigmoid(a)) * b).astype(BF16)
    cw = o_ref.shape[1] // n_col
    for n in range(n_col):
        cols = slice(n * cw, (n + 1) * cw)
        o_ref[:, cols] += jnp.dot(act, wd_ref[:, cols], preferred_element_type=F32)


def _ffn_half(x, g, wg, wu, wd):
    t, d = x.shape
    tf = FF_CHUNK
    n_ff = wd.shape[0] // tf
    tm = t if t <= FFN_ROW_TILE else FFN_ROW_TILE
    return pl.pallas_call(
        functools.partial(_ffn_kernel, n_col=4),
        grid=(pl.cdiv(t, tm), n_ff),
        in_specs=[
            pl.BlockSpec((tm, d), lambda i, j: (i, 0)),
            pl.BlockSpec((1, d), lambda i, j: (0, 0)),
            pl.BlockSpec((d, tf), lambda i, j: (0, j)),
            pl.BlockSpec((d, tf), lambda i, j: (0, j)),
            pl.BlockSpec((tf, d), lambda i, j: (j, 0)),
        ],
        out_specs=pl.BlockSpec((tm, d), lambda i, j: (i, 0)),
        out_shape=jax.ShapeDtypeStruct((t, d), F32),
        scratch_shapes=[pltpu.VMEM((tm, d), BF16)],
        compiler_params=_cparams(("parallel", "arbitrary")),
        name="ffn_half",
    )(x, g.reshape(1, d), wg, wu, wd)


def _prep_ffn_weights(w_gu, w_down):
    f = w_gu.shape[1] // 2
    pad = FF_CHUNK * pl.cdiv(f, FF_CHUNK) - f
    wg = jnp.pad(w_gu[:, :f].astype(BF16), ((0, 0), (0, pad)))
    wu = jnp.pad(w_gu[:, f:].astype(BF16), ((0, 0), (0, pad)))
    wd = jnp.pad(w_down.astype(BF16), ((0, pad), (0, 0)))
    return wg, wu, wd


def _inproj_kernel(x_ref, g_ref, w_ref, hg_ref, p_ref, k_ref, v_ref, xn_ref, *, n_heads, hd):
    j = pl.program_id(1)

    @pl.when(j == 0)
    def _():
        xn_ref[...] = _rms_rows(x_ref[...], g_ref[...]).astype(BF16)

    heads = [slice(h * hd, (h + 1) * hd) for h in range(n_heads)]

    def projected_heads():
        per_group = max(1, MXU_COLS // hd)
        for h0 in range(0, n_heads, per_group):
            hs = range(h0, min(h0 + per_group, n_heads))
            r = jnp.dot(xn_ref[...], w_ref[:, hs[0] * hd:(hs[-1] + 1) * hd], preferred_element_type=F32)
            for h in hs:
                yield h, heads[h], r[:, (h - h0) * hd:(h - h0 + 1) * hd]

    @pl.when(j == 1)
    def _():
        for h, cols, ph in projected_heads():
            p_ref[:, cols] = _rms_rows(ph, hg_ref[:, cols])

    @pl.when(j == 2)
    def _():
        k = jnp.concatenate([_rms_rows(ph, hg_ref[:, cols]) for _, cols, ph in projected_heads()], axis=1)
        p_ref[...] = k
        k_ref[...] = pltpu.einshape("t(hd)->thd", k, h=n_heads).reshape(k_ref.shape)

    @pl.when(j == 3)
    def _():
        v = jnp.dot(xn_ref[...], w_ref[...], preferred_element_type=F32)
        p_ref[...] = v
        v_ref[...] = pltpu.einshape("t(hd)->thd", v, h=n_heads).reshape(v_ref.shape)

    @pl.when(jnp.logical_or(j == 0, j > 3))
    def _():
        p_ref[...] = jnp.dot(xn_ref[...], w_ref[...], preferred_element_type=F32)


def _in_proj(x, g_mix, w_in, head_gain, n_heads, hd, seq_layout=None):
    t, d = x.shape
    in_w = w_in.shape[1]
    tn = n_heads * hd
    tm = t if t <= FFN_ROW_TILE else FFN_ROW_TILE
    if seq_layout is None:
        kv_spec = pl.BlockSpec((tm, n_heads, hd), lambda i, j: (i, 0, 0))
        kv_shape = jax.ShapeDtypeStruct((t, n_heads, hd), F32)
    else:
        n_seq, lead = seq_layout
        rows = t // n_seq
        assert rows % tm == 0
        per_seq = rows // tm
        kv_spec = pl.BlockSpec((pl.Element(1), pl.Element(tm), pl.Element(n_heads), pl.Element(hd)),
                               lambda i, j: (i // per_seq, lead + lax.rem(i, per_seq) * tm, 0, 0))
        kv_shape = jax.ShapeDtypeStruct((n_seq, lead + rows, n_heads, hd), F32)
    return pl.pallas_call(
        functools.partial(_inproj_kernel, n_heads=n_heads, hd=hd),
        grid=(pl.cdiv(t, tm), in_w // tn),
        in_specs=[
            pl.BlockSpec((tm, d), lambda i, j: (i, 0)),
            pl.BlockSpec((1, d), lambda i, j: (0, 0)),
            pl.BlockSpec((d, tn), lambda i, j: (0, j)),
            pl.BlockSpec((None, 1, tn), lambda i, j: (j, 0, 0)),
        ],
        out_specs=[pl.BlockSpec((tm, tn), lambda i, j: (i, j)), kv_spec, kv_spec],
        out_shape=[jax.ShapeDtypeStruct((t, in_w), F32), kv_shape, kv_shape],
        scratch_shapes=[pltpu.VMEM((tm, d), BF16)],
        compiler_params=_cparams(("parallel", "arbitrary")),
        name="in_proj",
    )(x, g_mix.reshape(1, d), w_in, head_gain)


def _fill_lead_kernel(lead_ref, seq_hbm_ref, o_ref):
    del seq_hbm_ref
    o_ref[...] = lead_ref[...]


def _fill_lead(seqs, lead_rows):
    n_seq, _, n_heads, hd = seqs.shape
    lead = lead_rows.shape[0]
    return pl.pallas_call(
        _fill_lead_kernel,
        grid=(n_seq,),
        in_specs=[pl.BlockSpec((lead, n_heads, hd), lambda bi: (0, 0, 0)),
                  pl.BlockSpec(memory_space=pl.ANY)],
        out_specs=pl.BlockSpec((None, lead, n_heads, hd), lambda bi: (bi, 0, 0, 0)),
        out_shape=jax.ShapeDtypeStruct(seqs.shape, seqs.dtype),
        input_output_aliases={1: 0},
        compiler_params=_cparams(("parallel",)),
        name="fill_lead",
    )(lead_rows, seqs)


def _pool_kernel(u_ref, halo_ref, hist_ref, wp_ref, ps_ref, o_ref, *, tm, pos0, group):
    i = pl.program_id(1)
    prev = jnp.where(i == 0, hist_ref[...], halo_ref[...])
    pos = pos0 + i * tm + lax.broadcasted_iota(jnp.int32, (tm, 1), 0)
    for gi, w in enumerate(POOL_WINDOWS):
        cols = slice(gi * group, (gi + 1) * group)
        x = jnp.concatenate([prev[:, cols], u_ref[:, cols]], axis=0)
        s = x
        span = 1
        while span < w:
            s = s + pltpu.roll(s, span, axis=0)
            span *= 2
        count = jnp.minimum(w, pos + 1).astype(F32)
        d = s[HIST_ROWS:] / count - x[HIST_ROWS:]
        y = jnp.dot(d.astype(BF16), wp_ref[gi], preferred_element_type=F32)
        o_ref[:, cols] = (y * ps_ref[:, cols]).astype(o_ref.dtype)


def _pool_mixer(p3, hist, w_pool, pool_scale, pos0, pw):
    b, s, _ = p3.shape
    tm = s if s <= POOL_ROW_TILE else POOL_ROW_TILE
    assert s % tm == 0
    n_groups = w_pool.shape[0]
    group = pw // n_groups
    halo_blocks = tm // HIST_ROWS
    per_batch = hist.shape[0] != 1
    return pl.pallas_call(
        functools.partial(_pool_kernel, tm=tm, pos0=pos0, group=group),
        grid=(b, s // tm),
        in_specs=[
            pl.BlockSpec((None, tm, pw), lambda bi, i: (bi, i, 0)),
            pl.BlockSpec((None, HIST_ROWS, pw), lambda bi, i: (bi, jnp.maximum(i * halo_blocks - 1, 0), 0)),
            pl.BlockSpec((None, HIST_ROWS, pw), (lambda bi, i: (bi, 0, 0)) if per_batch else (lambda bi, i: (0, 0, 0))),
            pl.BlockSpec((n_groups, group, group), lambda bi, i: (0, 0, 0)),
            pl.BlockSpec((1, pw), lambda bi, i: (0, 0)),
        ],
        out_specs=pl.BlockSpec((None, tm, pw), lambda bi, i: (bi, i, 0)),
        out_shape=jax.ShapeDtypeStruct((b, s, pw), BF16),
        compiler_params=_cparams(("parallel", "arbitrary")),
        name="pool_mixer",
    )(p3, p3, hist, w_pool, pool_scale.reshape(1, pw))


LOG2E = 1.4426950408889634
MASKED_LOGIT = -1e30


EXP2_CLAMP = 126.0


def _neg_log2_stay(z):
    return jnp.maximum(jnp.log2(1.0 + jnp.exp2(jnp.minimum(z, EXP2_CLAMP))), z)


def _sb_block(qb, kb, vb, tri, bias2, mask):
    z = lax.dot_general(qb, kb, (((1,), (1,)), ((), ())), preferred_element_type=F32) + bias2
    nls = _neg_log2_stay(z)
    nls_vis = nls if mask is None else jnp.where(mask, nls, 0.0)
    later = jnp.dot(nls_vis.astype(BF16), tri, preferred_element_type=F32)
    w = jnp.exp2(z - nls - later)
    if mask is not None:
        w = jnp.where(mask, w, 0.0)
    pv = jnp.dot(w.astype(BF16), vb, preferred_element_type=F32)
    csum = later[:, :1] + nls_vis[:, :1]
    return csum, pv


def _sb_chunk(qb, kc, vc, tri, bias_tile):
    m = qb.shape[0]
    tb = tri.shape[0]
    n_sub = kc.shape[0] // tb
    z = lax.dot_general(qb, kc, (((1,), (1,)), ((), ())), preferred_element_type=F32)
    z = z + jnp.concatenate([bias_tile] * n_sub, axis=1)
    nls = _neg_log2_stay(z)
    nls_rows = jnp.concatenate([nls[:, j * tb:(j + 1) * tb] for j in range(n_sub)], axis=0)
    later_rows = jnp.dot(nls_rows.astype(BF16), tri, preferred_element_type=F32)
    sub_sum = later_rows[:, :1] + nls_rows[:, :1]
    parts = []
    tail = jnp.zeros((m, 1), F32)
    for j in reversed(range(n_sub)):
        parts.append(later_rows[j * m:(j + 1) * m] + tail)
        tail = tail + sub_sum[j * m:(j + 1) * m]
    later = jnp.concatenate(parts[::-1], axis=1)
    w = jnp.exp2(z - nls - later)
    pv = jnp.dot(w.astype(BF16), vc, preferred_element_type=F32)
    return tail, pv


def _sb_fold(c, acc, csum, pv):
    return c + csum, acc + jnp.exp2(-c) * pv


def _prompt_tile(h, i, bias_ref, q_ref, tri_ref, o_ref, kb_ref, vb_ref, kmb_ref, vmb_ref, tk, hd):
    tq = q_ref.shape[0]
    nb = tq // tk
    bias2 = bias_ref[h] * LOG2E
    qb = (q_ref[...] * (hd ** -0.5 * LOG2E)).astype(BF16)
    tri = tri_ref[...]

    def kv(j):
        st = pl.multiple_of(j * tk, tk)
        return kb_ref[pl.ds(st, tk), :], vb_ref[pl.ds(st, tk), :]

    nm = kmb_ref.shape[0]
    zt = lax.dot_general(kmb_ref[...], qb, (((1,), (1,)), ((), ())), preferred_element_type=F32) + bias2
    nlst = _neg_log2_stay(zt)
    later_keys = (lax.broadcasted_iota(jnp.int32, (nm, nm), 0) < lax.broadcasted_iota(jnp.int32, (nm, nm), 1))
    latert = jnp.dot(jnp.where(later_keys, 1.0, 0.0).astype(BF16), nlst.astype(BF16), preferred_element_type=F32)
    wt = jnp.exp2(zt - nlst - latert).astype(BF16)
    pv_meta = lax.dot_general(wt, vmb_ref[...], (((0,), (0,)), ((), ())), preferred_element_type=F32)

    lower = lax.broadcasted_iota(jnp.int32, (tk, tk), 1) < lax.broadcasted_iota(jnp.int32, (tk, tk), 0)
    state = [None] * nb
    for jj in reversed(range(nb)):
        kj, vj = kv(nb * i + jj)
        state[jj] = _sb_block(qb[jj * tk:(jj + 1) * tk], kj, vj, tri, bias2, lower)
        if jj + 1 < nb:
            cs, pv = _sb_block(qb[(jj + 1) * tk:], kj, vj, tri, bias2, None)
            for r in range(jj + 1, nb):
                rows = slice((r - jj - 1) * tk, (r - jj) * tk)
                state[r] = _sb_fold(*state[r], cs[rows], pv[rows])
    c = jnp.concatenate([st[0] for st in state], axis=0)
    acc = jnp.concatenate([st[1] for st in state], axis=0)

    def body(t, carry):
        blocks = [_sb_block(qb, *kv(nb * (i - t) - 1 - u), tri, bias2, None) for u in range(nb)]
        for blk in blocks:
            carry = _sb_fold(*carry, *blk)
        return carry

    c, acc = lax.fori_loop(0, i, body, (c, acc))
    o_ref[...] = (acc + jnp.exp2(-c) * pv_meta).astype(o_ref.dtype)


def _attention(p3, pm, q, k_new, v_new, cache_k, cache_v, page_table, layer, sb_bias):
    b, s, _ = p3.shape
    db, dt, n_heads, hd = q.shape
    aw = n_heads * hd
    tk = ATTN_TILE
    tq = ATTN_Q_BLOCKS * tk
    nm = pm.shape[0]
    n_pages = page_table.shape[1]
    tb = tk
    assert s % tq == 0 and tb == 2 * PAGE_SIZE
    assert n_pages % PAGES_PER_STEP == 0 and tb % n_heads == 0 and dt * n_heads <= tb
    n_steps = n_pages // PAGES_PER_STEP
    grid = (b, n_heads, s // tq)
    total = db * n_steps
    assert total <= grid[0] * grid[1] * grid[2], "more sample steps than prompt tiles to pair them with"
    qo, ko, vo = aw // hd, 2 * aw // hd, 3 * aw // hd
    tri = (lax.broadcasted_iota(jnp.int32, (tk, tk), 0) > lax.broadcasted_iota(jnp.int32, (tk, tk), 1)).astype(BF16)

    m = n_heads * dt
    qm = (q * (hd ** -0.5 * LOG2E)).transpose(0, 2, 1, 3).reshape(db, m, hd).astype(BF16)
    row_h = jnp.arange(m) // dt
    row_q = jnp.arange(m) % dt
    col_h = jnp.arange(tb) % n_heads
    col_t = jnp.arange(tb) // n_heads
    same_head = row_h[:, None] == col_h[None, :]
    bias_rows = jnp.repeat(sb_bias * LOG2E, dt)[:, None]
    bias_past = jnp.where(same_head, bias_rows, MASKED_LOGIT)
    bias_new = jnp.where(same_head & (col_t[None, :] < row_q[:, None]), bias_rows, MASKED_LOGIT)
    kn = k_new.reshape(db, dt * n_heads, hd)
    vn = v_new.reshape(db, dt * n_heads, hd)

    def sample_step(bi, h, i):
        n = jnp.minimum((bi * grid[1] + h) * grid[2] + i, total - 1)
        return n // n_steps, lax.rem(n, n_steps)

    def seq_block(bi, h, i, pt):
        return (sample_step(bi, h, i)[0], 0, 0)

    def page_spec(r):
        def index(bi, h, i, pt):
            seq, st = sample_step(bi, h, i)
            return (layer, pt[seq * n_pages + (n_pages - (st + 1) * PAGES_PER_STEP + r)], 0, 0, 0)
        return pl.BlockSpec((None, None, PAGE_SIZE, n_heads, hd), index)

    grid_spec = pltpu.PrefetchScalarGridSpec(
        num_scalar_prefetch=1,
        grid=grid,
        in_specs=[
            pl.BlockSpec(memory_space=pltpu.SMEM),
            pl.BlockSpec((None, tq, hd), lambda bi, h, i, pt: (bi, i, qo + h)),
            pl.BlockSpec((None, s, hd), lambda bi, h, i, pt: (bi, 0, ko + h)),
            pl.BlockSpec((None, s, hd), lambda bi, h, i, pt: (bi, 0, vo + h)),
            pl.BlockSpec((nm, hd), lambda bi, h, i, pt: (0, ko + h)),
            pl.BlockSpec((nm, hd), lambda bi, h, i, pt: (0, vo + h)),
            pl.BlockSpec((tk, tk), lambda bi, h, i, pt: (0, 0)),
            pl.BlockSpec((None, m, hd), seq_block),
            pl.BlockSpec((m, tb), lambda bi, h, i, pt: (0, 0)),
            pl.BlockSpec((m, tb), lambda bi, h, i, pt: (0, 0)),
            pl.BlockSpec((None, dt * n_heads, hd), seq_block),
            pl.BlockSpec((None, dt * n_heads, hd), seq_block),
        ] + [page_spec(r) for r in range(PAGES_PER_STEP)] * 2,
        out_specs=[pl.BlockSpec((None, tq, hd), lambda bi, h, i, pt: (bi, i, h)),
                   pl.BlockSpec((None, m, hd), seq_block)],
        scratch_shapes=[pltpu.VMEM((s, hd), BF16), pltpu.VMEM((s, hd), BF16),
                        pltpu.VMEM((nm, hd), BF16), pltpu.VMEM((nm, hd), BF16),
                        pltpu.VMEM((m, 1), F32), pltpu.VMEM((m, hd), F32)],
    )
    o_p, o_s = pl.pallas_call(
        functools.partial(_attn_kernel, tk=tk, hd=hd, grid=grid, n_steps=n_steps, n_seq=db),
        grid_spec=grid_spec,
        out_shape=[jax.ShapeDtypeStruct((b, s, aw), BF16), jax.ShapeDtypeStruct((db, m, hd), F32)],
        compiler_params=_cparams(("arbitrary", "arbitrary", "arbitrary")),
        name="attention",
    )(page_table.reshape(-1), sb_bias, p3, p3, p3, pm, pm, tri, qm, bias_past, bias_new, kn, vn,
      *([cache_k] * PAGES_PER_STEP), *([cache_v] * PAGES_PER_STEP))
    return o_p, o_s.reshape(db, n_heads, dt, hd).transpose(0, 2, 1, 3)


def _sample_start(q_ref, bias_new_ref, kn_ref, vn_ref, tri_ref, c_ref, acc_ref):
    tri = tri_ref[...]
    zpad = jnp.zeros((tri.shape[0] - kn_ref.shape[0], kn_ref.shape[1]), BF16)
    kn = jnp.concatenate([kn_ref[...].astype(BF16), zpad], axis=0)
    vn = jnp.concatenate([vn_ref[...].astype(BF16), zpad], axis=0)
    c_ref[...], acc_ref[...] = _sb_chunk(q_ref[...], kn, vn, tri, bias_new_ref[...])


def _sample_step(active, q_ref, bias_past_ref, tri_ref, page_refs, c_ref, acc_ref):
    def flat(ref):
        return ref[...].reshape(-1, ref.shape[-1]).astype(BF16)

    kc = jnp.concatenate([flat(page_refs[r]) for r in range(PAGES_PER_STEP)], axis=0)
    vc = jnp.concatenate([flat(page_refs[PAGES_PER_STEP + r]) for r in range(PAGES_PER_STEP)], axis=0)
    c, acc = _sb_fold(c_ref[...], acc_ref[...], *_sb_chunk(q_ref[...], kc, vc, tri_ref[...], bias_past_ref[...]))
    if active is not None:
        c = jnp.where(active, c, c_ref[...])
        acc = jnp.where(active, acc, acc_ref[...])
    c_ref[...] = c
    acc_ref[...] = acc


def _attn_kernel(pt_ref, bias_ref, q_ref, k_ref, v_ref, km_ref, vm_ref, tri_ref,
                 sq_ref, bias_past_ref, bias_new_ref, kn_ref, vn_ref, *rest, tk, hd, grid, n_steps, n_seq):
    page_refs = rest[:2 * PAGES_PER_STEP]
    o_ref, so_ref, kb_ref, vb_ref, kmb_ref, vmb_ref, c_ref, acc_ref = rest[2 * PAGES_PER_STEP:]
    h = pl.program_id(1)
    i = pl.program_id(2)
    n = (pl.program_id(0) * grid[1] + h) * grid[2] + i
    total = n_seq * n_steps
    active = None if total == grid[0] * grid[1] * grid[2] else n < total
    st = lax.rem(jnp.minimum(n, total - 1), n_steps)

    @pl.when(i == 0)
    def _():
        kb_ref[...] = k_ref[...].astype(BF16)
        vb_ref[...] = v_ref[...].astype(BF16)
        kmb_ref[...] = km_ref[...].astype(BF16)
        vmb_ref[...] = vm_ref[...].astype(BF16)

    @pl.when(st == 0 if active is None else jnp.logical_and(st == 0, active))
    def _():
        _sample_start(sq_ref, bias_new_ref, kn_ref, vn_ref, tri_ref, c_ref, acc_ref)

    _sample_step(active, sq_ref, bias_past_ref, tri_ref, page_refs, c_ref, acc_ref)
    _prompt_tile(h, i, bias_ref, q_ref, tri_ref, o_ref, kb_ref, vb_ref, kmb_ref, vmb_ref, tk, hd)

    @pl.when(st == n_steps - 1)
    def _():
        so_ref[...] = acc_ref[...]


def _merge_kernel(x_ref, a_ref, o_ref, gp_ref, ga_ref, wbp_ref, wba_ref, wo_ref, y_ref):
    bp = jnp.dot(a_ref[...].astype(BF16), wbp_ref[...], preferred_element_type=F32)
    ba = jnp.dot(o_ref[...].astype(BF16), wba_ref[...], preferred_element_type=F32)
    m = jax.nn.sigmoid(gp_ref[...]) * bp + jax.nn.sigmoid(ga_ref[...]) * ba
    y_ref[...] = x_ref[...] + jnp.dot(m.astype(BF16), wo_ref[...], preferred_element_type=F32)


def _merge(x, a, o, p, w_bp, w_ba, w_out, gate_block):
    t, d = x.shape
    pw, aw = a.shape[1], o.shape[1]
    tm = _row_tile(t)
    const = functools.partial(pl.BlockSpec, pipeline_mode=pl.Buffered(1))
    return pl.pallas_call(
        _merge_kernel,
        grid=(pl.cdiv(t, tm),),
        in_specs=[
            pl.BlockSpec((tm, d), lambda i: (i, 0)),
            pl.BlockSpec((tm, pw), lambda i: (i, 0)),
            pl.BlockSpec((tm, aw), lambda i: (i, 0)),
            pl.BlockSpec((tm, d), lambda i: (i, gate_block)),
            pl.BlockSpec((tm, d), lambda i: (i, gate_block + 1)),
            const((pw, d), lambda i: (0, 0)),
            const((aw, d), lambda i: (0, 0)),
            const((d, d), lambda i: (0, 0)),
        ],
        out_specs=pl.BlockSpec((tm, d), lambda i: (i, 0)),
        out_shape=jax.ShapeDtypeStruct((t, d), F32),
        compiler_params=_cparams(("parallel",)),
        name="merge",
    )(x, a, o, p, p, w_bp, w_ba, w_out)


def kernel(x_prompt, x_sample, cache_k, cache_v, state_pool, page_table, meta_tokens, g_ffn1, w_ffn1_gu, w_ffn1_down, g_mix, w_in, g_q, g_k, sb_bias, w_pool, pool_scale, w_branch_pool, w_branch_attn, w_out, g_ffn2, w_ffn2_gu, w_ffn2_down):
    b, s, d = x_prompt.shape
    db, dt, _ = x_sample.shape
    depth, n_heads = sb_bias.shape
    hd = g_q.shape[1]
    aw = n_heads * hd
    pw = state_pool.shape[-1]
    in_w = w_in.shape[-1]
    n_meta = meta_tokens.shape[0]
    past_len = page_table.shape[1] * PAGE_SIZE
    assert depth == 1, "meta rows are dropped after the mixer; a second layer would need them"
    assert pw == aw and in_w == pw + 3 * aw + 2 * d and n_meta == N_META
    l = 0

    ffn1 = _prep_ffn_weights(w_ffn1_gu[l], w_ffn1_down[l])
    ffn2 = _prep_ffn_weights(w_ffn2_gu[l], w_ffn2_down[l])
    w_in_b = w_in[l].astype(BF16)
    w_bp = w_branch_pool[l].astype(BF16)
    w_ba = w_branch_attn[l].astype(BF16)
    w_o = w_out[l].astype(BF16)
    w_pl = w_pool[l].astype(BF16)
    n_chunks = in_w // aw
    head_gain = jnp.ones((n_chunks, 1, aw), F32)
    head_gain = head_gain.at[1, 0].set(jnp.tile(g_q[l], n_heads)).at[2, 0].set(jnp.tile(g_k[l], n_heads))
    gate_block = (pw + 3 * aw) // d

    xr = x_prompt.reshape(b * s, d)
    xe = jnp.concatenate([meta_tokens.astype(F32), x_sample.reshape(db * dt, d)], axis=0)

    x1r = _ffn_half(xr, g_ffn1[l], *ffn1)
    x1e = _ffn_half(xe, g_ffn1[l], *ffn1)
    pr, kp, vp = _in_proj(x1r, g_mix[l], w_in_b, head_gain, n_heads, hd, seq_layout=(b, n_meta))
    pe, ke, ve = _in_proj(x1e, g_mix[l], w_in_b, head_gain, n_heads, hd)
    pm = pe[:n_meta]
    ps = pe[n_meta:]
    p3 = pr.reshape(b, s, pr.shape[1])

    q_s = ps[:, pw:pw + aw].reshape(db, dt, n_heads, hd)
    k_s = ke[n_meta:].reshape(db, dt, n_heads, hd)
    v_s = ve[n_meta:].reshape(db, dt, n_heads, hd)
    o_p, o_s = _attention(p3, pm, q_s, k_s, v_s, cache_k, cache_v, page_table, l, sb_bias[l])

    zrow = jnp.zeros((1, pw), F32)
    hist_p = jnp.concatenate([zrow, pm[n_meta - POOL_HIST:, :pw]], axis=0)[None]
    a_p = _pool_mixer(p3, hist_p, w_pl, pool_scale[l], n_meta, pw)
    x2r = _merge(x1r, a_p.reshape(b * s, pw), o_p.reshape(b * s, aw), pr, w_bp, w_ba, w_o, gate_block)
    y_prompt = _ffn_half(x2r, g_ffn2[l], *ffn2).reshape(b, s, d)

    u_s = ps[:, :pw].reshape(db, dt, pw)
    rows_s = HIST_ROWS * pl.cdiv(dt, HIST_ROWS)
    u_s_pad = jnp.pad(u_s, ((0, 0), (0, rows_s - dt), (0, 0)))
    hist_s = jnp.concatenate([jnp.zeros((db, 1, pw), F32), state_pool[l]], axis=1)
    a_s = _pool_mixer(u_s_pad, hist_s, w_pl, pool_scale[l], past_len, pw)[:, :dt]
    x1s = x1e[n_meta:]
    x2s = _merge(x1s, a_s.reshape(db * dt, pw), o_s.reshape(db * dt, aw).astype(BF16), ps, w_bp, w_ba, w_o, gate_block)
    y_sample = _ffn_half(x2s, g_ffn2[l], *ffn2).reshape(db, dt, d)

    k_prompt = _fill_lead(kp, ke[:n_meta])[None]
    v_prompt = _fill_lead(vp, ve[:n_meta])[None]
    pool_prompt = p3[:, s - POOL_HIST:, :pw][None]
    pool_sample = jnp.concatenate([state_pool[l], u_s], axis=1)[:, -POOL_HIST:][None]
    return (y_prompt, y_sample, k_prompt, v_prompt, pool_prompt,
            k_s[None], v_s[None], pool_sample)
```

```python
import functools

import jax
import jax.numpy as jnp
from jax import lax
from jax.experimental import pallas as pl
from jax.experimental.pallas import tpu as pltpu

F32 = jnp.float32
BF16 = jnp.bfloat16

EPS = 1e-6
N_META = 16
POOL_WINDOWS = (2, 4, 8, 16)
POOL_HIST = max(POOL_WINDOWS) - 1
HIST_ROWS = 16
PAGE_SIZE = 128
PAGES_PER_STEP = 16
FF_CHUNK = 512
ROW_TILE = 512
FFN_ROW_TILE = 1024
POOL_ROW_TILE = 1024
MXU_COLS = 256
ATTN_TILE = 256
ATTN_Q_BLOCKS = 4
VMEM_LIMIT = 60 * 1024 * 1024


def _cparams(sem):
    return pltpu.CompilerParams(dimension_semantics=sem, vmem_limit_bytes=VMEM_LIMIT)


def _row_tile(t):
    return t if t <= ROW_TILE else ROW_TILE


def _rms_rows(x, g):
    ms = jnp.mean(x * x, axis=-1, keepdims=True)
    return x * lax.rsqrt(ms + EPS) * g


def _ffn_kernel(x_ref, g_ref, wg_ref, wu_ref, wd_ref, o_ref, xn_ref, *, n_col):
    j = pl.program_id(1)

    @pl.when(j == 0)
    def _():
        x = x_ref[...]
        xn_ref[...] = _rms_rows(x, g_ref[...]).astype(BF16)
        o_ref[...] = x

    xn = xn_ref[...]
    a = jnp.dot(xn, wg_ref[...], preferred_element_type=F32)
    b = jnp.dot(xn, wu_ref[...], preferred_element_type=F32)
    act = (((0.5 * a) * jax.nn.sigmoid(a)) * b).astype(BF16)
    cw = o_ref.shape[1] // n_col
    for n in range(n_col):
        cols = slice(n * cw, (n + 1) * cw)
        o_ref[:, cols] += jnp.dot(act, wd_ref[:, cols], preferred_element_type=F32)


def _ffn_half(x, g, wg, wu, wd):
    t, d = x.shape
    tf = FF_CHUNK
    n_ff = wd.shape[0] // tf
    tm = t if t <= FFN_ROW_TILE else FFN_ROW_TILE
    return pl.pallas_call(
        functools.partial(_ffn_kernel, n_col=4),
        grid=(pl.cdiv(t, tm), n_ff),
        in_specs=[
            pl.BlockSpec((tm, d), lambda i, j: (i, 0)),
            pl.BlockSpec((1, d), lambda i, j: (0, 0)),
            pl.BlockSpec((d, tf), lambda i, j: (0, j)),
            pl.BlockSpec((d, tf), lambda i, j: (0, j)),
            pl.BlockSpec((tf, d), lambda i, j: (j, 0)),
        ],
        out_specs=pl.BlockSpec((tm, d), lambda i, j: (i, 0)),
        out_shape=jax.ShapeDtypeStruct((t, d), F32),
        scratch_shapes=[pltpu.VMEM((tm, d), BF16)],
        compiler_params=_cparams(("parallel", "arbitrary")),
        name="ffn_half",
    )(x, g.reshape(1, d), wg, wu, wd)


def _prep_ffn_weights(w_gu, w_down):
    f = w_gu.shape[1] // 2
    pad = FF_CHUNK * pl.cdiv(f, FF_CHUNK) - f
    wg = jnp.pad(w_gu[:, :f].astype(BF16), ((0, 0), (0, pad)))
    wu = jnp.pad(w_gu[:, f:].astype(BF16), ((0, 0), (0, pad)))
    wd = jnp.pad(w_down.astype(BF16), ((0, pad), (0, 0)))
    return wg, wu, wd


def _inproj_kernel(x_ref, g_ref, w_ref, hg_ref, *rest, n_heads, hd):
    p_ref, k_ref, v_ref, xn_ref = rest[-4:]
    j = pl.program_id(1)

    @pl.when(j == 0)
    def _():
        xn_ref[...] = _rms_rows(x_ref[...], g_ref[...]).astype(BF16)

    heads = [slice(h * hd, (h + 1) * hd) for h in range(n_heads)]

    def projected_heads():
        per_group = max(1, MXU_COLS // hd)
        for h0 in range(0, n_heads, per_group):
            hs = range(h0, min(h0 + per_group, n_heads))
            r = jnp.dot(xn_ref[...], w_ref[:, hs[0] * hd:(hs[-1] + 1) * hd], preferred_element_type=F32)
            for h in hs:
                yield h, heads[h], r[:, (h - h0) * hd:(h - h0 + 1) * hd]

    @pl.when(j == 1)
    def _():
        for h, cols, ph in projected_heads():
            p_ref[:, cols] = _rms_rows(ph, hg_ref[:, cols])

    @pl.when(j == 2)
    def _():
        k = jnp.concatenate([_rms_rows(ph, hg_ref[:, cols]) for _, cols, ph in projected_heads()], axis=1)
        p_ref[...] = k
        k_ref[...] = pltpu.einshape("t(hd)->thd", k, h=n_heads).reshape(k_ref.shape)

    @pl.when(j == 3)
    def _():
        v = jnp.dot(xn_ref[...], w_ref[...], preferred_element_type=F32)
        p_ref[...] = v
        v_ref[...] = pltpu.einshape("t(hd)->thd", v, h=n_heads).reshape(v_ref.shape)

    @pl.when(jnp.logical_or(j == 0, j > 3))
    def _():
        p_ref[...] = jnp.dot(xn_ref[...], w_ref[...], preferred_element_type=F32)


def _in_proj(x, g_mix, w_in, head_gain, n_heads, hd, seq_layout=None, kv_init=()):
    assert (seq_layout is None) == (not kv_init)
    t, d = x.shape
    in_w = w_in.shape[1]
    tn = n_heads * hd
    tm = t if t <= FFN_ROW_TILE else FFN_ROW_TILE
    if seq_layout is None:
        kv_spec = pl.BlockSpec((tm, n_heads, hd), lambda i, j: (i, 0, 0))
        kv_shape = jax.ShapeDtypeStruct((t, n_heads, hd), F32)
    else:
        n_seq, lead = seq_layout
        rows = t // n_seq
        assert rows % tm == 0
        per_seq = rows // tm
        kv_spec = pl.BlockSpec((pl.Element(1), pl.Element(tm), pl.Element(n_heads), pl.Element(hd)),
                               lambda i, j: (i // per_seq, lead + lax.rem(i, per_seq) * tm, 0, 0))
        kv_shape = jax.ShapeDtypeStruct((n_seq, lead + rows, n_heads, hd), F32)
    return pl.pallas_call(
        functools.partial(_inproj_kernel, n_heads=n_heads, hd=hd),
        grid=(pl.cdiv(t, tm), in_w // tn),
        in_specs=[
            pl.BlockSpec((tm, d), lambda i, j: (i, 0)),
            pl.BlockSpec((1, d), lambda i, j: (0, 0)),
            pl.BlockSpec((d, tn), lambda i, j: (0, j)),
            pl.BlockSpec((None, 1, tn), lambda i, j: (j, 0, 0)),
        ] + [pl.BlockSpec(memory_space=pl.ANY)] * len(kv_init),
        out_specs=[pl.BlockSpec((tm, tn), lambda i, j: (i, j)), kv_spec, kv_spec],
        out_shape=[jax.ShapeDtypeStruct((t, in_w), F32), kv_shape, kv_shape],
        input_output_aliases={4: 1, 5: 2} if kv_init else {},
        scratch_shapes=[pltpu.VMEM((tm, d), BF16)],
        compiler_params=_cparams(("parallel", "arbitrary")),
        name="in_proj",
    )(x, g_mix.reshape(1, d), w_in, head_gain, *kv_init)


def _pool_kernel(u_ref, halo_ref, hist_ref, wp_ref, ps_ref, o_ref, *, tm, pos0, group):
    i = pl.program_id(1)
    prev = jnp.where(i == 0, hist_ref[...], halo_ref[...])
    pos = pos0 + i * tm + lax.broadcasted_iota(jnp.int32, (tm, 1), 0)
    for gi, w in enumerate(POOL_WINDOWS):
        cols = slice(gi * group, (gi + 1) * group)
        x = jnp.concatenate([prev[:, cols], u_ref[:, cols]], axis=0)
        s = x
        span = 1
        while span < w:
            s = s + pltpu.roll(s, span, axis=0)
            span *= 2
        count = jnp.minimum(w, pos + 1).astype(F32)
        d = s[HIST_ROWS:] / count - x[HIST_ROWS:]
        y = jnp.dot(d.astype(BF16), wp_ref[gi], preferred_element_type=F32)
        o_ref[:, cols] = (y * ps_ref[:, cols]).astype(o_ref.dtype)


def _pool_mixer(p3, hist, w_pool, pool_scale, pos0, pw):
    b, s, _ = p3.shape
    tm = s if s <= POOL_ROW_TILE else POOL_ROW_TILE
    assert s % tm == 0
    n_groups = w_pool.shape[0]
    group = pw // n_groups
    halo_blocks = tm // HIST_ROWS
    per_batch = hist.shape[0] != 1
    return pl.pallas_call(
        functools.partial(_pool_kernel, tm=tm, pos0=pos0, group=group),
        grid=(b, s // tm),
        in_specs=[
            pl.BlockSpec((None, tm, pw), lambda bi, i: (bi, i, 0)),
            pl.BlockSpec((None, HIST_ROWS, pw), lambda bi, i: (bi, jnp.maximum(i * halo_blocks - 1, 0), 0)),
            pl.BlockSpec((None, HIST_ROWS, pw), (lambda bi, i: (bi, 0, 0)) if per_batch else (lambda bi, i: (0, 0, 0))),
            pl.BlockSpec((n_groups, group, group), lambda bi, i: (0, 0, 0)),
            pl.BlockSpec((1, pw), lambda bi, i: (0, 0)),
        ],
        out_specs=pl.BlockSpec((None, tm, pw), lambda bi, i: (bi, i, 0)),
        out_shape=jax.ShapeDtypeStruct((b, s, pw), BF16),
        compiler_params=_cparams(("parallel", "arbitrary")),
        name="pool_mixer",
    )(p3, p3, hist, w_pool, pool_scale.reshape(1, pw))


LOG2E = 1.4426950408889634
MASKED_LOGIT = -1e30


EXP2_CLAMP = 126.0


def _neg_log2_stay(z):
    return jnp.maximum(jnp.log2(1.0 + jnp.exp2(jnp.minimum(z, EXP2_CLAMP))), z)


def _sb_block(qb, kb, vb, tri, bias2, mask):
    z = lax.dot_general(qb, kb, (((1,), (1,)), ((), ())), preferred_element_type=F32) + bias2
    nls = _neg_log2_stay(z)
    nls_vis = nls if mask is None else jnp.where(mask, nls, 0.0)
    later = jnp.dot(nls_vis.astype(BF16), tri, preferred_element_type=F32)
    w = jnp.exp2(z - nls - later)
    if mask is not None:
        w = jnp.where(mask, w, 0.0)
    pv = jnp.dot(w.astype(BF16), vb, preferred_element_type=F32)
    csum = later[:, :1] + nls_vis[:, :1]
    return csum, pv


def _sb_chunk(qb, kc, vc, tri, bias_tile):
    m = qb.shape[0]
    tb = tri.shape[0]
    n_sub = kc.shape[0] // tb
    z = lax.dot_general(qb, kc, (((1,), (1,)), ((), ())), preferred_element_type=F32)
    z = z + jnp.concatenate([bias_tile] * n_sub, axis=1)
    nls = _neg_log2_stay(z)
    nls_rows = jnp.concatenate([nls[:, j * tb:(j + 1) * tb] for j in range(n_sub)], axis=0)
    later_rows = jnp.dot(nls_rows.astype(BF16), tri, preferred_element_type=F32)
    sub_sum = later_rows[:, :1] + nls_rows[:, :1]
    parts = []
    tail = jnp.zeros((m, 1), F32)
    for j in reversed(range(n_sub)):
        parts.append(later_rows[j * m:(j + 1) * m] + tail)
        tail = tail + sub_sum[j * m:(j + 1) * m]
    later = jnp.concatenate(parts[::-1], axis=1)
    w = jnp.exp2(z - nls - later)
    pv = jnp.dot(w.astype(BF16), vc, preferred_element_type=F32)
    return tail, pv


def _sb_fold(c, acc, csum, pv):
    return c + csum, acc + jnp.exp2(-c) * pv


def _prompt_tile(h, i, bias_ref, q_ref, tri_ref, o_ref, kb_ref, vb_ref, kmb_ref, vmb_ref, tk, hd):
    tq = q_ref.shape[0]
    nb = tq // tk
    bias2 = bias_ref[h] * LOG2E
    qb = (q_ref[...] * (hd ** -0.5 * LOG2E)).astype(BF16)
    tri = tri_ref[...]

    def kv(j):
        st = pl.multiple_of(j * tk, tk)
        return kb_ref[pl.ds(st, tk), :], vb_ref[pl.ds(st, tk), :]

    nm = kmb_ref.shape[0]
    zt = lax.dot_general(kmb_ref[...], qb, (((1,), (1,)), ((), ())), preferred_element_type=F32) + bias2
    nlst = _neg_log2_stay(zt)
    later_keys = (lax.broadcasted_iota(jnp.int32, (nm, nm), 0) < lax.broadcasted_iota(jnp.int32, (nm, nm), 1))
    latert = jnp.dot(jnp.where(later_keys, 1.0, 0.0).astype(BF16), nlst.astype(BF16), preferred_element_type=F32)
    wt = jnp.exp2(zt - nlst - latert).astype(BF16)
    pv_meta = lax.dot_general(wt, vmb_ref[...], (((0,), (0,)), ((), ())), preferred_element_type=F32)

    lower = lax.broadcasted_iota(jnp.int32, (tk, tk), 1) < lax.broadcasted_iota(jnp.int32, (tk, tk), 0)
    state = [None] * nb
    for jj in reversed(range(nb)):
        kj, vj = kv(nb * i + jj)
        state[jj] = _sb_block(qb[jj * tk:(jj + 1) * tk], kj, vj, tri, bias2, lower)
        if jj + 1 < nb:
            cs, pv = _sb_block(qb[(jj + 1) * tk:], kj, vj, tri, bias2, None)
            for r in range(jj + 1, nb):
                rows = slice((r - jj - 1) * tk, (r - jj) * tk)
                state[r] = _sb_fold(*state[r], cs[rows], pv[rows])
    c = jnp.concatenate([st[0] for st in state], axis=0)
    acc = jnp.concatenate([st[1] for st in state], axis=0)

    def body(t, carry):
        blocks = [_sb_block(qb, *kv(nb * (i - t) - 1 - u), tri, bias2, None) for u in range(nb)]
        for blk in blocks:
            carry = _sb_fold(*carry, *blk)
        return carry

    c, acc = lax.fori_loop(0, i, body, (c, acc))
    o_ref[...] = (acc + jnp.exp2(-c) * pv_meta).astype(o_ref.dtype)


def _attention(p3, pm, q, k_new, v_new, cache_k, cache_v, page_table, layer, sb_bias):
    b, s, _ = p3.shape
    db, dt, n_heads, hd = q.shape
    aw = n_heads * hd
    tk = ATTN_TILE
    tq = ATTN_Q_BLOCKS * tk
    nm = pm.shape[0]
    n_pages = page_table.shape[1]
    tb = tk
    assert s % tq == 0 and tb == 2 * PAGE_SIZE
    assert n_pages % PAGES_PER_STEP == 0 and tb % n_heads == 0 and dt * n_heads <= tb
    n_steps = n_pages // PAGES_PER_STEP
    grid = (b, n_heads, s // tq)
    total = db * n_steps
    assert total <= grid[0] * grid[1] * grid[2], "more sample steps than prompt tiles to pair them with"
    qo, ko, vo = aw // hd, 2 * aw // hd, 3 * aw // hd
    tri = (lax.broadcasted_iota(jnp.int32, (tk, tk), 0) > lax.broadcasted_iota(jnp.int32, (tk, tk), 1)).astype(BF16)

    m = n_heads * dt
    qm = (q * (hd ** -0.5 * LOG2E)).transpose(0, 2, 1, 3).reshape(db, m, hd).astype(BF16)
    row_h = jnp.arange(m) // dt
    row_q = jnp.arange(m) % dt
    col_h = jnp.arange(tb) % n_heads
    col_t = jnp.arange(tb) // n_heads
    same_head = row_h[:, None] == col_h[None, :]
    bias_rows = jnp.repeat(sb_bias * LOG2E, dt)[:, None]
    bias_past = jnp.where(same_head, bias_rows, MASKED_LOGIT)
    bias_new = jnp.where(same_head & (col_t[None, :] < row_q[:, None]), bias_rows, MASKED_LOGIT)
    kn = k_new.reshape(db, dt * n_heads, hd)
    vn = v_new.reshape(db, dt * n_heads, hd)

    def sample_step(bi, h, i):
        n = jnp.minimum((bi * grid[1] + h) * grid[2] + i, total - 1)
        return n // n_steps, lax.rem(n, n_steps)

    def seq_block(bi, h, i, pt):
        return (sample_step(bi, h, i)[0], 0, 0)

    def page_spec(r):
        def index(bi, h, i, pt):
            seq, st = sample_step(bi, h, i)
            return (layer, pt[seq * n_pages + (n_pages - (st + 1) * PAGES_PER_STEP + r)], 0, 0, 0)
        return pl.BlockSpec((None, None, PAGE_SIZE, n_heads, hd), index)

    grid_spec = pltpu.PrefetchScalarGridSpec(
        num_scalar_prefetch=1,
        grid=grid,
        in_specs=[
            pl.BlockSpec(memory_space=pltpu.SMEM),
            pl.BlockSpec((None, tq, hd), lambda bi, h, i, pt: (bi, i, qo + h)),
            pl.BlockSpec((None, s, hd), lambda bi, h, i, pt: (bi, 0, ko + h)),
            pl.BlockSpec((None, s, hd), lambda bi, h, i, pt: (bi, 0, vo + h)),
            pl.BlockSpec((nm, hd), lambda bi, h, i, pt: (0, ko + h)),
            pl.BlockSpec((nm, hd), lambda bi, h, i, pt: (0, vo + h)),
            pl.BlockSpec((tk, tk), lambda bi, h, i, pt: (0, 0)),
            pl.BlockSpec((None, m, hd), seq_block),
            pl.BlockSpec((m, tb), lambda bi, h, i, pt: (0, 0)),
            pl.BlockSpec((m, tb), lambda bi, h, i, pt: (0, 0)),
            pl.BlockSpec((None, dt * n_heads, hd), seq_block),
            pl.BlockSpec((None, dt * n_heads, hd), seq_block),
        ] + [page_spec(r) for r in range(PAGES_PER_STEP)] * 2,
        out_specs=[pl.BlockSpec((None, tq, hd), lambda bi, h, i, pt: (bi, i, h)),
                   pl.BlockSpec((None, m, hd), seq_block)],
        scratch_shapes=[pltpu.VMEM((s, hd), BF16), pltpu.VMEM((s, hd), BF16),
                        pltpu.VMEM((nm, hd), BF16), pltpu.VMEM((nm, hd), BF16),
                        pltpu.VMEM((m, 1), F32), pltpu.VMEM((m, hd), F32)],
    )
    o_p, o_s = pl.pallas_call(
        functools.partial(_attn_kernel, tk=tk, hd=hd, grid=grid, n_steps=n_steps, n_seq=db),
        grid_spec=grid_spec,
        out_shape=[jax.ShapeDtypeStruct((b, s, aw), BF16), jax.ShapeDtypeStruct((db, m, hd), F32)],
        compiler_params=_cparams(("arbitrary", "arbitrary", "arbitrary")),
        name="attention",
    )(page_table.reshape(-1), sb_bias, p3, p3, p3, pm, pm, tri, qm, bias_past, bias_new, kn, vn,
      *([cache_k] * PAGES_PER_STEP), *([cache_v] * PAGES_PER_STEP))
    return o_p, o_s.reshape(db, n_heads, dt, hd).transpose(0, 2, 1, 3)


def _sample_start(q_ref, bias_new_ref, kn_ref, vn_ref, tri_ref, c_ref, acc_ref):
    tri = tri_ref[...]
    zpad = jnp.zeros((tri.shape[0] - kn_ref.shape[0], kn_ref.shape[1]), BF16)
    kn = jnp.concatenate([kn_ref[...].astype(BF16), zpad], axis=0)
    vn = jnp.concatenate([vn_ref[...].astype(BF16), zpad], axis=0)
    c_ref[...], acc_ref[...] = _sb_chunk(q_ref[...], kn, vn, tri, bias_new_ref[...])


def _sample_step(active, q_ref, bias_past_ref, tri_ref, page_refs, c_ref, acc_ref):
    def flat(ref):
        return ref[...].reshape(-1, ref.shape[-1]).astype(BF16)

    kc = jnp.concatenate([flat(page_refs[r]) for r in range(PAGES_PER_STEP)], axis=0)
    vc = jnp.concatenate([flat(page_refs[PAGES_PER_STEP + r]) for r in range(PAGES_PER_STEP)], axis=0)
    c, acc = _sb_fold(c_ref[...], acc_ref[...], *_sb_chunk(q_ref[...], kc, vc, tri_ref[...], bias_past_ref[...]))
    if active is not None:
        c = jnp.where(active, c, c_ref[...])
        acc = jnp.where(active, acc, acc_ref[...])
    c_ref[...] = c
    acc_ref[...] = acc


def _attn_kernel(pt_ref, bias_ref, q_ref, k_ref, v_ref, km_ref, vm_ref, tri_ref,
                 sq_ref, bias_past_ref, bias_new_ref, kn_ref, vn_ref, *rest, tk, hd, grid, n_steps, n_seq):
    page_refs = rest[:2 * PAGES_PER_STEP]
    o_ref, so_ref, kb_ref, vb_ref, kmb_ref, vmb_ref, c_ref, acc_ref = rest[2 * PAGES_PER_STEP:]
    h = pl.program_id(1)
    i = pl.program_id(2)
    n = (pl.program_id(0) * grid[1] + h) * grid[2] + i
    total = n_seq * n_steps
    active = None if total == grid[0] * grid[1] * grid[2] else n < total
    st = lax.rem(jnp.minimum(n, total - 1), n_steps)

    @pl.when(i == 0)
    def _():
        kb_ref[...] = k_ref[...].astype(BF16)
        vb_ref[...] = v_ref[...].astype(BF16)
        kmb_ref[...] = km_ref[...].astype(BF16)
        vmb_ref[...] = vm_ref[...].astype(BF16)

    @pl.when(st == 0 if active is None else jnp.logical_and(st == 0, active))
    def _():
        _sample_start(sq_ref, bias_new_ref, kn_ref, vn_ref, tri_ref, c_ref, acc_ref)

    _sample_step(active, sq_ref, bias_past_ref, tri_ref, page_refs, c_ref, acc_ref)
    _prompt_tile(h, i, bias_ref, q_ref, tri_ref, o_ref, kb_ref, vb_ref, kmb_ref, vmb_ref, tk, hd)

    @pl.when(st == n_steps - 1)
    def _():
        so_ref[...] = acc_ref[...]


def _merge_kernel(x_ref, a_ref, o_ref, gp_ref, ga_ref, wbp_ref, wba_ref, wo_ref, y_ref):
    bp = jnp.dot(a_ref[...].astype(BF16), wbp_ref[...], preferred_element_type=F32)
    ba = jnp.dot(o_ref[...].astype(BF16), wba_ref[...], preferred_element_type=F32)
    m = jax.nn.sigmoid(gp_ref[...]) * bp + jax.nn.sigmoid(ga_ref[...]) * ba
    y_ref[...] = x_ref[...] + jnp.dot(m.astype(BF16), wo_ref[...], preferred_element_type=F32)


def _merge(x, a, o, p, w_bp, w_ba, w_out, gate_block):
    t, d = x.shape
    pw, aw = a.shape[1], o.shape[1]
    tm = _row_tile(t)
    const = functools.partial(pl.BlockSpec, pipeline_mode=pl.Buffered(1))
    return pl.pallas_call(
        _merge_kernel,
        grid=(pl.cdiv(t, tm),),
        in_specs=[
            pl.BlockSpec((tm, d), lambda i: (i, 0)),
            pl.BlockSpec((tm, pw), lambda i: (i, 0)),
            pl.BlockSpec((tm, aw), lambda i: (i, 0)),
            pl.BlockSpec((tm, d), lambda i: (i, gate_block)),
            pl.BlockSpec((tm, d), lambda i: (i, gate_block + 1)),
            const((pw, d), lambda i: (0, 0)),
            const((aw, d), lambda i: (0, 0)),
            const((d, d), lambda i: (0, 0)),
        ],
        out_specs=pl.BlockSpec((tm, d), lambda i: (i, 0)),
        out_shape=jax.ShapeDtypeStruct((t, d), F32),
        compiler_params=_cparams(("parallel",)),
        name="merge",
    )(x, a, o, p, p, w_bp, w_ba, w_out)


def kernel(x_prompt, x_sample, cache_k, cache_v, state_pool, page_table, meta_tokens, g_ffn1, w_ffn1_gu, w_ffn1_down, g_mix, w_in, g_q, g_k, sb_bias, w_pool, pool_scale, w_branch_pool, w_branch_attn, w_out, g_ffn2, w_ffn2_gu, w_ffn2_down):
    b, s, d = x_prompt.shape
    db, dt, _ = x_sample.shape
    depth, n_heads = sb_bias.shape
    hd = g_q.shape[1]
    aw = n_heads * hd
    pw = state_pool.shape[-1]
    in_w = w_in.shape[-1]
    n_meta = meta_tokens.shape[0]
    past_len = page_table.shape[1] * PAGE_SIZE
    assert depth == 1, "meta rows are dropped after the mixer; a second layer would need them"
    assert pw == aw and in_w == pw + 3 * aw + 2 * d and n_meta == N_META
    l = 0

    ffn1 = _prep_ffn_weights(w_ffn1_gu[l], w_ffn1_down[l])
    ffn2 = _prep_ffn_weights(w_ffn2_gu[l], w_ffn2_down[l])
    w_in_b = w_in[l].astype(BF16)
    w_bp = w_branch_pool[l].astype(BF16)
    w_ba = w_branch_attn[l].astype(BF16)
    w_o = w_out[l].astype(BF16)
    w_pl = w_pool[l].astype(BF16)
    n_chunks = in_w // aw
    head_gain = jnp.ones((n_chunks, 1, aw), F32)
    head_gain = head_gain.at[1, 0].set(jnp.tile(g_q[l], n_heads)).at[2, 0].set(jnp.tile(g_k[l], n_heads))
    gate_block = (pw + 3 * aw) // d

    xr = x_prompt.reshape(b * s, d)
    xe = jnp.concatenate([meta_tokens.astype(F32), x_sample.reshape(db * dt, d)], axis=0)

    x1r = _ffn_half(xr, g_ffn1[l], *ffn1)
    x1e = _ffn_half(xe, g_ffn1[l], *ffn1)
    pe, ke, ve = _in_proj(x1e, g_mix[l], w_in_b, head_gain, n_heads, hd)

    def meta_first(meta):
        return jnp.pad(jnp.broadcast_to(meta[None, :n_meta], (b, n_meta, n_heads, hd)), ((0, 0), (0, s), (0, 0), (0, 0)))

    pr, k_prompt, v_prompt = _in_proj(x1r, g_mix[l], w_in_b, head_gain, n_heads, hd, seq_layout=(b, n_meta),
                                      kv_init=(meta_first(ke), meta_first(ve)))
    pm = pe[:n_meta]
    ps = pe[n_meta:]
    p3 = pr.reshape(b, s, pr.shape[1])

    q_s = ps[:, pw:pw + aw].reshape(db, dt, n_heads, hd)
    k_s = ke[n_meta:].reshape(db, dt, n_heads, hd)
    v_s = ve[n_meta:].reshape(db, dt, n_heads, hd)
    o_p, o_s = _attention(p3, pm, q_s, k_s, v_s, cache_k, cache_v, page_table, l, sb_bias[l])

    zrow = jnp.zeros((1, pw), F32)
    hist_p = jnp.concatenate([zrow, pm[n_meta - POOL_HIST:, :pw]], axis=0)[None]
    a_p = _pool_mixer(p3, hist_p, w_pl, pool_scale[l], n_meta, pw)
    x2r = _merge(x1r, a_p.reshape(b * s, pw), o_p.reshape(b * s, aw), pr, w_bp, w_ba, w_o, gate_block)
    y_prompt = _ffn_half(x2r, g_ffn2[l], *ffn2).reshape(b, s, d)

    u_s = ps[:, :pw].reshape(db, dt, pw)
    rows_s = HIST_ROWS * pl.cdiv(dt, HIST_ROWS)
    u_s_pad = jnp.pad(u_s, ((0, 0), (0, rows_s - dt), (0, 0)))
    hist_s = jnp.concatenate([jnp.zeros((db, 1, pw), F32), state_pool[l]], axis=1)
    a_s = _pool_mixer(u_s_pad, hist_s, w_pl, pool_scale[l], past_len, pw)[:, :dt]
    x1s = x1e[n_meta:]
    x2s = _merge(x1s, a_s.reshape(db * dt, pw), o_s.reshape(db * dt, aw).astype(BF16), ps, w_bp, w_ba, w_o, gate_block)
    y_sample = _ffn_half(x2s, g_ffn2[l], *ffn2).reshape(db, dt, d)

    pool_prompt = p3[:, s - POOL_HIST:, :pw][None]
    pool_sample = jnp.concatenate([state_pool[l], u_s], axis=1)[:, -POOL_HIST:][None]
    return (y_prompt, y_sample, k_prompt[None], v_prompt[None], pool_prompt,
            k_s[None], v_s[None], pool_sample)
```
